```python
import math
import jax
import jax.numpy as jnp
from jax import lax
import numpy as np

D_MODEL = 1024
BATCH = 8
SEQ = 4096
DEPTH = 2
DEC_BATCH = 32
DEC_SEQ = 1
PAST_LEN = 16384
PAGE_SIZE = 128

ATT_HEADS = 4
ATT_DK = D_MODEL // 16
ATT_DV = 2 * ATT_DK
ATT_WIDTH = ATT_HEADS * ATT_DV
Q_BLOCK = 128
LRU_WIDTH = D_MODEL // 4
LRU_BLOCKS = 4
LRU_BLOCK = LRU_WIDTH // LRU_BLOCKS
LRU_CONV = 4
LRU_C = 8.0
CCM_WIDTH = D_MODEL // 4
CCM_CONV = 31
MIX_WIDTH = ATT_WIDTH + LRU_WIDTH + CCM_WIDTH
IN_WIDTH = 4 * ATT_HEADS * ATT_DK + ATT_WIDTH + 2 * LRU_WIDTH + 2 * CCM_WIDTH
N_GROUPS = 4
EXPERTS_PER_GROUP = 8
N_EXPERTS = N_GROUPS * EXPERTS_PER_GROUP
TOP_K = 2
D_EXPERT = D_MODEL // 2
MOE_BLOCK = 128

EPS = 1e-6
F32 = jnp.float32

kernel_name = 'hymba_style_diffattn_rglru_conformer_hmoe_step'


def rms_norm(x, g):
    x32 = x.astype(F32)
    y = x32 * lax.rsqrt(jnp.mean(x32 * x32, axis=-1, keepdims=True) + EPS)
    return (y * g.astype(F32)).astype(x.dtype)


def layer_norm(x, g, b):
    x32 = x.astype(F32)
    xc = x32 - jnp.mean(x32, axis=-1, keepdims=True)
    var = jnp.mean(xc * xc, axis=-1, keepdims=True)
    return (xc * lax.rsqrt(var + EPS) * g.astype(F32) + b.astype(F32)).astype(x.dtype)


def causal_dwconv(x, buf, w, b):
    width, ch = w.shape
    xe = jnp.concatenate([buf.astype(x.dtype), x], axis=1)
    y = lax.conv_general_dilated(xe, w[:, None, :].astype(x.dtype), window_strides=(1,), padding='VALID',
                                 dimension_numbers=('NWC', 'WIO', 'NWC'), feature_group_count=ch)
    return y + b, xe[:, xe.shape[1] - (width - 1):]


def split_mix(p):
    sizes = (2 * ATT_HEADS * ATT_DK, 2 * ATT_HEADS * ATT_DK, ATT_WIDTH, LRU_WIDTH, LRU_WIDTH, CCM_WIDTH, CCM_WIDTH)
    idx = []
    acc = 0
    for s in sizes[:-1]:
        acc += s
        idx.append(acc)
    return jnp.split(p, idx, axis=-1)


def diff_lambda(lq1, lk1, lq2, lk2, lam_init):
    return (jnp.exp(jnp.sum(lq1.astype(F32) * lk1.astype(F32)))
            - jnp.exp(jnp.sum(lq2.astype(F32) * lk2.astype(F32))) + lam_init)


def diff_weights(s, mask, lam):
    p = jax.nn.softmax(jnp.where(mask, s.astype(F32), -jnp.inf), axis=-1)
    return p[:, 0] - lam * p[:, 1]


def diff_attn_prompt(q, k, v, lam):
    B_, S_ = q.shape[:2]
    nb = S_ // Q_BLOCK
    scale = ATT_DK ** -0.5
    qb = jnp.swapaxes(q.reshape(B_, nb, Q_BLOCK, ATT_HEADS, 2, ATT_DK), 0, 1)
    k_pos = jnp.arange(S_)

    def block(args):
        qi, i = args
        q_pos = i * Q_BLOCK + jnp.arange(Q_BLOCK)
        mask = k_pos[None, :] <= q_pos[:, None]
        s = jnp.einsum('bqhmd,bkhmd->bmhqk', qi, k) * scale
        w = diff_weights(s, mask, lam).astype(v.dtype)
        return jnp.einsum('bhqk,bkhe->bqhe', w, v)

    o = lax.map(block, (qb, jnp.arange(nb)))
    return jnp.swapaxes(o, 0, 1).reshape(B_, S_, ATT_HEADS, ATT_DV)


def diff_attn_sample(q, k, v, k_pages, v_pages, page_table, lam):
    DB, T = q.shape[:2]
    scale = ATT_DK ** -0.5
    kp = k_pages[page_table].reshape(DB, -1, ATT_HEADS, 2, ATT_DK).astype(q.dtype)
    vp = v_pages[page_table].reshape(DB, -1, ATT_HEADS, ATT_DV).astype(v.dtype)
    P = kp.shape[1]
    s_past = jnp.einsum('bqhmd,bkhmd->bmhqk', q, kp) * scale
    s_new = jnp.einsum('bqhmd,bkhmd->bmhqk', q, k) * scale
    mask = jnp.concatenate([jnp.ones((T, P), bool), jnp.tril(jnp.ones((T, T), bool))], axis=1)
    w = diff_weights(jnp.concatenate([s_past, s_new], axis=-1), mask, lam).astype(v.dtype)
    return (jnp.einsum('bhqk,bkhe->bqhe', w[..., :P], vp)
            + jnp.einsum('bhqk,bkhe->bqhe', w[..., P:], v))


def diff_head_out(o, g, lam_init):
    o = rms_norm(o, g) * (1.0 - lam_init)
    return o.reshape(o.shape[0], o.shape[1], ATT_WIDTH)


def rglru_branch(xr, xg, buf, h0, conv_w, conv_b, wa, ba, wx, bx, lam):
    B_, L_, C = xr.shape
    xc, new_buf = causal_dwconv(xr, buf, conv_w, conv_b)
    xb = xc.reshape(B_, L_, LRU_BLOCKS, LRU_BLOCK)
    r = jax.nn.sigmoid(jnp.einsum('blnc,ncd->blnd', xb, wa).reshape(B_, L_, C) + ba).astype(F32)
    i = jax.nn.sigmoid(jnp.einsum('blnc,ncd->blnd', xb, wx).reshape(B_, L_, C) + bx).astype(F32)
    log_a = -LRU_C * r * jax.nn.softplus(-lam.astype(F32))
    a = jnp.exp(log_a)
    u = jnp.sqrt(-jnp.expm1(2.0 * log_a)) * i * xc.astype(F32)

    def step(h, au):
        h = au[0] * h + au[1]
        return h, h

    h_last, hs = lax.scan(step, h0.astype(F32), (jnp.swapaxes(a, 0, 1), jnp.swapaxes(u, 0, 1)))
    y = jnp.swapaxes(hs, 0, 1).astype(xr.dtype) * jax.nn.gelu(xg)
    return y, h_last.astype(xr.dtype), new_buf


def conformer_conv(ca, cb, buf, dw_w, dw_b, ln_g, ln_b):
    glu = ca * jax.nn.sigmoid(cb)
    c, new_buf = causal_dwconv(glu, buf, dw_w, dw_b)
    c = layer_norm(c, ln_g, ln_b)
    return jax.nn.silu(c), new_buf


def moe_dispatch(xf, expert_idx, gates, w_gate, w_up, w_down):
    T, K = expert_idx.shape
    E = w_gate.shape[0]
    A = T * K
    n_blocks = -(-A // MOE_BLOCK) + E
    cap = n_blocks * MOE_BLOCK
    flat_e = expert_idx.reshape(A).astype(jnp.int32)
    order = jnp.argsort(flat_e).astype(jnp.int32)
    sorted_e = flat_e[order]
    counts = jnp.bincount(flat_e, length=E).astype(jnp.int32)
    padded = (counts + MOE_BLOCK - 1) // MOE_BLOCK * MOE_BLOCK
    pad_end = jnp.cumsum(padded)
    pad_start = pad_end - padded
    start = jnp.cumsum(counts) - counts
    dest = pad_start[sorted_e] + jnp.arange(A, dtype=jnp.int32) - start[sorted_e]
    slot_tok = jnp.full((cap,), T, jnp.int32).at[dest].set(order // K)
    slot_w = jnp.zeros((cap,), F32).at[dest].set(gates.reshape(A).astype(F32)[order])
    block_e = jnp.minimum(jnp.searchsorted(pad_end, jnp.arange(n_blocks, dtype=jnp.int32) * MOE_BLOCK, side='right'), E - 1)
    xpad = jnp.concatenate([xf, jnp.zeros((1, xf.shape[1]), xf.dtype)], axis=0)

    def run_block(args):
        tok, e = args
        hb = xpad[tok]
        return (jax.nn.silu(hb @ w_gate[e]) * (hb @ w_up[e])) @ w_down[e]

    yb = lax.map(run_block, (slot_tok.reshape(n_blocks, MOE_BLOCK), block_e))
    yb = yb.reshape(cap, xf.shape[1]) * slot_w[:, None].astype(xf.dtype)
    return jnp.zeros_like(xpad).at[slot_tok].add(yb)[:T]


def hier_moe(h, rg_w, rg_b, re_w, re_b, w_gate, w_up, w_down):
    B_, L_, D_ = h.shape
    hf = h.reshape(B_ * L_, D_)
    g_logits = (hf @ rg_w + rg_b).astype(F32)
    g_sel = jnp.argmax(g_logits, axis=-1)
    p_g = jnp.take_along_axis(jax.nn.softmax(g_logits, axis=-1), g_sel[:, None], axis=-1)
    e_logits = (jnp.einsum('td,gde->tge', hf, re_w) + re_b).astype(F32)
    e_logits = jnp.take_along_axis(e_logits, g_sel[:, None, None], axis=1)[:, 0]
    top_v, top_i = lax.top_k(e_logits, TOP_K)
    gates = jax.nn.softmax(top_v, axis=-1) * p_g
    expert_idx = g_sel[:, None].astype(jnp.int32) * EXPERTS_PER_GROUP + top_i.astype(jnp.int32)
    y = moe_dispatch(hf, expert_idx, gates, w_gate, w_up, w_down)
    return y.reshape(B_, L_, D_)


def setup_inputs(seed: int = 0) -> dict:
    key = jax.random.key(seed)
    keys = iter(jax.random.split(key, 48))
    n_pages = PAST_LEN // PAGE_SIZE
    n_used = DEC_BATCH * n_pages
    n_pool = n_used + n_used // 4

    def normal(shape, scale):
        return scale * jax.random.normal(next(keys), shape, F32)

    u = jax.random.uniform(next(keys), (DEPTH, LRU_WIDTH), F32, 0.9, 0.999)
    a = u ** (1.0 / LRU_C)
    lru_lambda = jnp.log(a) - jnp.log1p(-a)
    page_table = jax.random.permutation(next(keys), n_pool)[:n_used].reshape(DEC_BATCH, n_pages).astype(jnp.int32)
    return {
        'x_prompt': normal((BATCH, SEQ, D_MODEL), 1.0),
        'x_sample': normal((DEC_BATCH, DEC_SEQ, D_MODEL), 1.0),
        'cache_k': normal((DEPTH, n_pool, PAGE_SIZE, ATT_HEADS, 2 * ATT_DK), 1.0),
        'cache_v': normal((DEPTH, n_pool, PAGE_SIZE, ATT_HEADS, ATT_DV), 1.0),
        'state_lru_h': normal((DEPTH, DEC_BATCH, LRU_WIDTH), 0.5),
        'state_lru_conv': normal((DEPTH, DEC_BATCH, LRU_CONV - 1, LRU_WIDTH), 1.0),
        'state_ccm_conv': normal((DEPTH, DEC_BATCH, CCM_CONV - 1, CCM_WIDTH), 0.5),
        'page_table': page_table,
        'norm_mix_g': 1.0 + normal((DEPTH, D_MODEL), 0.05),
        'w_in': normal((DEPTH, D_MODEL, IN_WIDTH), D_MODEL ** -0.5),
        'lam_q1': normal((DEPTH, ATT_DK), 0.1),
        'lam_k1': normal((DEPTH, ATT_DK), 0.1),
        'lam_q2': normal((DEPTH, ATT_DK), 0.1),
        'lam_k2': normal((DEPTH, ATT_DK), 0.1),
        'attn_subln_g': 1.0 + normal((DEPTH, ATT_DV), 0.05),
        'lru_conv_w': normal((DEPTH, LRU_CONV, LRU_WIDTH), LRU_CONV ** -0.5),
        'lru_conv_b': normal((DEPTH, LRU_WIDTH), 0.02),
        'lru_wa': normal((DEPTH, LRU_BLOCKS, LRU_BLOCK, LRU_BLOCK), LRU_BLOCK ** -0.5),
        'lru_ba': normal((DEPTH, LRU_WIDTH), 0.1),
        'lru_wx': normal((DEPTH, LRU_BLOCKS, LRU_BLOCK, LRU_BLOCK), LRU_BLOCK ** -0.5),
        'lru_bx': normal((DEPTH, LRU_WIDTH), 0.1),
        'lru_lambda': lru_lambda,
        'ccm_dw_w': normal((DEPTH, CCM_CONV, CCM_WIDTH), CCM_CONV ** -0.5),
        'ccm_dw_b': normal((DEPTH, CCM_WIDTH), 0.02),
        'ccm_ln_g': 1.0 + normal((DEPTH, CCM_WIDTH), 0.05),
        'ccm_ln_b': normal((DEPTH, CCM_WIDTH), 0.02),
        'w_out': normal((DEPTH, MIX_WIDTH, D_MODEL), MIX_WIDTH ** -0.5),
        'norm_ffn_g': 1.0 + normal((DEPTH, D_MODEL), 0.05),
        'router_group_w': normal((DEPTH, D_MODEL, N_GROUPS), D_MODEL ** -0.5),
        'router_group_b': normal((DEPTH, N_GROUPS), 0.01),
        'router_expert_w': normal((DEPTH, N_GROUPS, D_MODEL, EXPERTS_PER_GROUP), D_MODEL ** -0.5),
        'router_expert_b': normal((DEPTH, N_GROUPS, EXPERTS_PER_GROUP), 0.01),
        'expert_w_gate': normal((DEPTH, N_EXPERTS, D_MODEL, D_EXPERT), D_MODEL ** -0.5),
        'expert_w_up': normal((DEPTH, N_EXPERTS, D_MODEL, D_EXPERT), D_MODEL ** -0.5),
        'expert_w_down': normal((DEPTH, N_EXPERTS, D_EXPERT, D_MODEL), D_EXPERT ** -0.5),
        'norm_final_g': 1.0 + normal((D_MODEL,), 0.05),
    }


def reference(x_prompt, x_sample, cache_k, cache_v, state_lru_h, state_lru_conv, state_ccm_conv, page_table,
              norm_mix_g, w_in, lam_q1, lam_k1, lam_q2, lam_k2, attn_subln_g,
              lru_conv_w, lru_conv_b, lru_wa, lru_ba, lru_wx, lru_bx, lru_lambda,
              ccm_dw_w, ccm_dw_b, ccm_ln_g, ccm_ln_b, w_out,
              norm_ffn_g, router_group_w, router_group_b, router_expert_w, router_expert_b,
              expert_w_gate, expert_w_up, expert_w_down, norm_final_g):

    def trunk(x, paged, lru_h, lru_conv, ccm_conv):
        B_, L_, _ = x.shape
        ks, vs, hs, lcs, ccs = [], [], [], [], []
        for l in range(DEPTH):
            lam_init = 0.8 - 0.6 * math.exp(-0.3 * l)
            lam = diff_lambda(lam_q1[l], lam_k1[l], lam_q2[l], lam_k2[l], lam_init)
            h = rms_norm(x, norm_mix_g[l])
            q, k, v, xr, xg, ca, cb = split_mix(h @ w_in[l])
            q = q.reshape(B_, L_, ATT_HEADS, 2, ATT_DK)
            k = k.reshape(B_, L_, ATT_HEADS, 2, ATT_DK)
            v = v.reshape(B_, L_, ATT_HEADS, ATT_DV)
            if paged is None:
                o = diff_attn_prompt(q, k, v, lam)
            else:
                o = diff_attn_sample(q, k, v, paged[0][l], paged[1][l], paged[2], lam)
            att = diff_head_out(o, attn_subln_g[l], lam_init)
            rec, h_new, lbuf = rglru_branch(xr, xg, lru_conv[l], lru_h[l], lru_conv_w[l], lru_conv_b[l],
                                            lru_wa[l], lru_ba[l], lru_wx[l], lru_bx[l], lru_lambda[l])
            cnv, cbuf = conformer_conv(ca, cb, ccm_conv[l], ccm_dw_w[l], ccm_dw_b[l], ccm_ln_g[l], ccm_ln_b[l])
            x = x + jnp.concatenate([att, rec, cnv], axis=-1) @ w_out[l]
            x = x + hier_moe(rms_norm(x, norm_ffn_g[l]), router_group_w[l], router_group_b[l],
                             router_expert_w[l], router_expert_b[l],
                             expert_w_gate[l], expert_w_up[l], expert_w_down[l])
            ks.append(k.reshape(B_, L_, ATT_HEADS, 2 * ATT_DK))
            vs.append(v)
            hs.append(h_new)
            lcs.append(lbuf)
            ccs.append(cbuf)
        return (rms_norm(x, norm_final_g), jnp.stack(ks), jnp.stack(vs), jnp.stack(hs), jnp.stack(lcs), jnp.stack(ccs))

    Bp = x_prompt.shape[0]
    dt = x_prompt.dtype
    y_prompt, k_prompt, v_prompt, lru_h_prompt, lru_conv_prompt, ccm_conv_prompt = trunk(
        x_prompt, None,
        jnp.zeros((DEPTH, Bp, LRU_WIDTH), dt),
        jnp.zeros((DEPTH, Bp, LRU_CONV - 1, LRU_WIDTH), dt),
        jnp.zeros((DEPTH, Bp, CCM_CONV - 1, CCM_WIDTH), dt))
    y_sample, k_sample, v_sample, lru_h_sample, lru_conv_sample, ccm_conv_sample = trunk(
        x_sample, (cache_k, cache_v, page_table), state_lru_h, state_lru_conv, state_ccm_conv)
    return (y_prompt, y_sample, k_prompt, v_prompt, lru_h_prompt, lru_conv_prompt, ccm_conv_prompt,
            k_sample, v_sample, lru_h_sample, lru_conv_sample, ccm_conv_sample)
```

```python
import functools
import math

import jax
import jax.numpy as jnp
from jax import lax
from jax.experimental import pallas as pl
from jax.experimental.pallas import tpu as pltpu

F32 = jnp.float32
BF16 = jnp.bfloat16
I32 = jnp.int32
EPS = 1e-6
LRU_C = 8.0
HEAD_W = 128
LANES = 128
SUBLANES = 8
MOE_ROWS = 256


def _cparams(sem, vmem_mb=48):
    return pltpu.CompilerParams(dimension_semantics=sem, vmem_limit_bytes=vmem_mb * 1024 * 1024)


def _full(shape):
    n = len(shape)
    return pl.BlockSpec(shape, lambda *_: (0,) * n)


def _rms(x, g):
    return x * lax.rsqrt(jnp.mean(x * x, axis=-1, keepdims=True) + EPS) * g


def _sigmoid(x):
    return jax.nn.sigmoid(x)


def _diff_lambda(lamv, lam_init):
    a = jnp.sum(lamv[0:1, :] * lamv[1:2, :], axis=-1, keepdims=True)
    b = jnp.sum(lamv[2:3, :] * lamv[3:4, :], axis=-1, keepdims=True)
    return jnp.exp(a) - jnp.exp(b) + lam_init


def _inproj_kernel(x_ref, g_ref, w_ref, qa_ref, qb_ref, k_ref, v_ref, kb_ref, vb_ref, r_ref, *, scale):
    h = _rms(x_ref[...], g_ref[...]).astype(BF16)
    aw = k_ref.shape[1]

    def mm(lo, hi):
        return jnp.dot(h, w_ref[:, lo:hi], preferred_element_type=F32)

    q = mm(0, aw) * scale
    lane = lax.broadcasted_iota(I32, q.shape, 1)
    first = (lane & (HEAD_W - 1)) < (HEAD_W // 2)
    qa_ref[...] = jnp.where(first, q, 0.0).astype(BF16)
    qb_ref[...] = jnp.where(first, 0.0, q).astype(BF16)
    k = mm(aw, 2 * aw)
    k_ref[...] = k
    kb_ref[...] = k.astype(BF16)
    v = mm(2 * aw, 3 * aw)
    v_ref[...] = v
    vb_ref[...] = v.astype(BF16)
    r_ref[...] = mm(3 * aw, w_ref.shape[1])


def _inproj(x, g, w_bf, aw, scale, tm):
    T, D = x.shape
    rw = w_bf.shape[1] - 3 * aw
    row = lambda w: pl.BlockSpec((tm, w), lambda i: (i, 0))
    return pl.pallas_call(
        functools.partial(_inproj_kernel, scale=scale),
        grid=(T // tm,),
        in_specs=[row(D), _full((1, D)), _full(w_bf.shape)],
        out_specs=[row(aw), row(aw), row(aw), row(aw), row(aw), row(aw), row(rw)],
        out_shape=[jax.ShapeDtypeStruct((T, aw), BF16), jax.ShapeDtypeStruct((T, aw), BF16),
                   jax.ShapeDtypeStruct((T, aw), F32), jax.ShapeDtypeStruct((T, aw), F32),
                   jax.ShapeDtypeStruct((T, aw), BF16), jax.ShapeDtypeStruct((T, aw), BF16),
                   jax.ShapeDtypeStruct((T, rw), F32)],
        compiler_params=_cparams(("parallel",)),
        name="inproj",
    )(x, g, w_bf)


def _attn_kernel(lamv_ref, g_ref, qa_ref, qb_ref, k_ref, v_ref, o_ref, m_ref, l_ref, acc_ref, *, tq, lam_init):
    i = pl.program_id(2)
    qq = jnp.concatenate([qa_ref[...], qb_ref[...]], axis=0)
    m_ref[...] = jnp.full(m_ref.shape, -jnp.inf, F32)
    l_ref[...] = jnp.zeros(l_ref.shape, F32)
    acc_ref[...] = jnp.zeros(acc_ref.shape, F32)

    def tile(j, masked):
        start = pl.multiple_of(j * tq, tq)
        kj = k_ref[pl.ds(start, tq), :]
        vj = v_ref[pl.ds(start, tq), :]
        s = lax.dot_general(qq, kj, (((1,), (1,)), ((), ())), preferred_element_type=F32)
        if masked:
            r = lax.broadcasted_iota(I32, s.shape, 0)
            c = lax.broadcasted_iota(I32, s.shape, 1)
            s = jnp.where(c <= jnp.where(r >= tq, r - tq, r), s, -jnp.inf)
        m_prev = m_ref[...]
        m_new = jnp.maximum(m_prev, jnp.max(s, axis=-1, keepdims=True))
        alpha = jnp.exp(m_prev - m_new)
        p = jnp.exp(s - m_new)
        l_ref[...] = alpha * l_ref[...] + jnp.sum(p, axis=-1, keepdims=True)
        acc_ref[...] = alpha * acc_ref[...] + jnp.dot(p.astype(BF16), vj, preferred_element_type=F32)
        m_ref[...] = m_new

    def body(j, c):
        tile(j, False)
        return c

    lax.fori_loop(0, i, body, 0)
    tile(i, True)

    lam = _diff_lambda(lamv_ref[...], lam_init)
    o = acc_ref[0:tq, :] / l_ref[0:tq, :] - lam * (acc_ref[tq:2 * tq, :] / l_ref[tq:2 * tq, :])
    o_ref[...] = (_rms(o, g_ref[...]) * (1.0 - lam_init)).astype(BF16)


def _attn(lamv, g, qa, qb, kb, vb, B, S, lam_init, tq):
    T, aw = qa.shape
    H = aw // HEAD_W
    nq = S // tq
    qspec = pl.BlockSpec((tq, HEAD_W), lambda b, h, i: (b * nq + i, h))
    kspec = pl.BlockSpec((S, HEAD_W), lambda b, h, i: (b, h))
    return pl.pallas_call(
        functools.partial(_attn_kernel, tq=tq, lam_init=lam_init),
        grid=(B, H, nq),
        in_specs=[_full(lamv.shape), _full((1, HEAD_W)), qspec, qspec, kspec, kspec],
        out_specs=qspec,
        out_shape=jax.ShapeDtypeStruct((T, aw), BF16),
        scratch_shapes=[pltpu.VMEM((2 * tq, 1), F32), pltpu.VMEM((2 * tq, 1), F32),
                        pltpu.VMEM((2 * tq, HEAD_W), F32)],
        compiler_params=_cparams(("parallel", "parallel", "arbitrary")),
        name="attn",
    )(lamv, g, qa, qb, kb, vb)


def _lru_gates(xc, wgate, bgate, llam):
    cw = xc.shape[-1]
    gz = jnp.dot(xc.astype(BF16), wgate, preferred_element_type=F32) + bgate
    r = _sigmoid(gz[:, 0:cw])
    ig = _sigmoid(gz[:, cw:2 * cw])
    log_a = -LRU_C * r * jax.nn.softplus(-llam)
    a = jnp.exp(log_a)
    u = jnp.sqrt(-jnp.tanh(log_a) * (a * a + 1.0)) * ig * xc
    return a, u


def _layer_norm_silu(c, g, b):
    mu = jnp.mean(c, axis=-1, keepdims=True)
    cc = c - mu
    var = jnp.mean(cc * cc, axis=-1, keepdims=True)
    y = cc * lax.rsqrt(var + EPS) * g + b
    return y * _sigmoid(y)


def _route(logits, n_groups, epg):
    lane = lax.broadcasted_iota(I32, logits.shape, 1).astype(F32)
    big = float(LANES)
    ninf = -jnp.inf
    gl = jnp.where(lane < n_groups, logits, ninf)
    gmax = jnp.max(gl, axis=-1, keepdims=True)
    gsel = jnp.min(jnp.where(gl == gmax, lane, big), axis=-1, keepdims=True)
    p_g = 1.0 / jnp.sum(jnp.exp(gl - gmax), axis=-1, keepdims=True)
    lo = n_groups + epg * gsel
    el = jnp.where((lane >= lo) & (lane < lo + epg), logits, ninf)
    v1 = jnp.max(el, axis=-1, keepdims=True)
    i1 = jnp.min(jnp.where(el == v1, lane, big), axis=-1, keepdims=True)
    el2 = jnp.where(lane == i1, ninf, el)
    v2 = jnp.max(el2, axis=-1, keepdims=True)
    i2 = jnp.min(jnp.where(el2 == v2, lane, big), axis=-1, keepdims=True)
    e2 = jnp.exp(v2 - v1)
    den = 1.0 + e2
    g1 = (1.0 / den) * p_g
    g2 = (e2 / den) * p_g
    return i1 - n_groups, i2 - n_groups, g1, g2, lane


def _pack_lanes(lane, vals):
    out = jnp.zeros(lane.shape, F32)
    for idx, v in enumerate(vals):
        out = jnp.where(lane == float(idx), v, out)
    return out


def _mix_kernel(att_ref, r_ref, x_ref, lcw_ref, lcb_ref, wgate_ref, bgate_ref, llam_ref,
                cdw_ref, cdb_ref, clg_ref, clb_ref, wo_ref, nfg_ref, wr_ref, br_ref,
                x1_ref, h2_ref, rec_ref, cnt_ref, hlast_ref, cst_ref,
                xe, ce, a_s, u_s, hs, hc, cnt_s, *, tm, n_groups, epg, lru_taps, ccm_taps):
    b = pl.program_id(0)
    t = pl.program_id(1)
    cw = xe.shape[1]
    lpad = xe.shape[0] - tm
    cpad = ce.shape[0] - tm

    @pl.when(t == 0)
    def _():
        xe[0:lpad, :] = jnp.zeros((lpad, cw), F32)
        ce[0:cpad, :] = jnp.zeros((cpad, cw), F32)
        hc[...] = jnp.zeros(hc.shape, F32)

    @pl.when((t == 0) & (b == 0))
    def _():
        cnt_s[...] = jnp.zeros(cnt_s.shape, F32)

    xg = r_ref[:, cw:2 * cw]
    xe[lpad:lpad + tm, :] = r_ref[:, 0:cw]
    xc = jnp.zeros((tm, cw), F32) + lcb_ref[...]
    for k in range(lru_taps):
        xc = xc + lcw_ref[k:k + 1, :] * xe[pl.ds(lpad - (lru_taps - 1) + k, tm), :]
    xe[0:lpad, :] = xe[tm:tm + lpad, :]
    a, u = _lru_gates(xc, wgate_ref[...], bgate_ref[...], llam_ref[...])
    rowm = lax.broadcasted_iota(I32, (tm, cw), 0) & (SUBLANES - 1)
    for s in (1, 2, 4):
        ok = rowm >= s
        u = jnp.where(ok, a * pltpu.roll(u, s, 0) + u, u)
        a = jnp.where(ok, a * pltpu.roll(a, s, 0), a)
    a_s[...] = a
    u_s[...] = u

    def grp(gi, h):
        o = pl.multiple_of(gi * SUBLANES, SUBLANES)
        h8 = a_s[pl.ds(o, SUBLANES), :] * h + u_s[pl.ds(o, SUBLANES), :]
        hs[pl.ds(o, SUBLANES), :] = h8
        return jnp.broadcast_to(h8[SUBLANES - 1:SUBLANES, :], (SUBLANES, cw))

    hfin = lax.fori_loop(0, tm // SUBLANES, grp, hc[...])
    hc[...] = hfin
    hlast_ref[...] = hfin[0:1, :]
    rec = hs[...] * jax.nn.gelu(xg)

    glu = r_ref[:, 2 * cw:3 * cw] * _sigmoid(r_ref[:, 3 * cw:4 * cw])
    ce[cpad:cpad + tm, :] = glu
    c = jnp.zeros((tm, cw), F32) + cdb_ref[...]
    for k in range(ccm_taps):
        c = c + cdw_ref[k:k + 1, :] * ce[pl.ds(cpad - (ccm_taps - 1) + k, tm), :]
    ce[0:cpad, :] = ce[tm:tm + cpad, :]
    cst_ref[...] = ce[0:cpad, :]
    cnv = _layer_norm_silu(c, clg_ref[...], clb_ref[...])

    aw = att_ref.shape[1]
    mixed = jnp.dot(att_ref[...], wo_ref[0:aw, :], preferred_element_type=F32)
    mixed += jnp.dot(rec.astype(BF16), wo_ref[aw:aw + cw, :], preferred_element_type=F32)
    mixed += jnp.dot(cnv.astype(BF16), wo_ref[aw + cw:aw + 2 * cw, :], preferred_element_type=F32)
    x1 = x_ref[...] + mixed
    x1_ref[...] = x1

    h2 = _rms(x1, nfg_ref[...])
    h2_ref[...] = h2
    logits = jnp.dot(h2.astype(BF16), wr_ref[...], preferred_element_type=F32) + br_ref[...]
    x1e, x2e, g1, g2, lane = _route(logits, n_groups, epg)
    oh0 = (lane == x1e).astype(F32)
    oh1 = (lane == x2e).astype(F32)
    both = oh0 + oh1
    rr = lax.broadcasted_iota(I32, (tm, tm), 0)
    cc = lax.broadcasted_iota(I32, (tm, tm), 1)
    tril = (cc < rr).astype(BF16)
    before = jnp.dot(tril, both.astype(BF16), preferred_element_type=F32) + cnt_s[0:1, :]
    rank0 = jnp.sum(oh0 * before, axis=-1, keepdims=True)
    rank1 = jnp.sum(oh1 * (before + oh0), axis=-1, keepdims=True)
    cnt_new = cnt_s[0:1, :] + jnp.sum(both, axis=0, keepdims=True)
    cnt_s[...] = jnp.broadcast_to(cnt_new, cnt_s.shape)
    cnt_ref[...] = cnt_s[...]
    rec_ref[...] = _pack_lanes(lane, [x1e, x2e, g1, g2, rank0, rank1])


def _mix(att, r4, x, p, B, S, tm, n_groups, epg):
    T, D = x.shape
    aw = att.shape[1]
    cw = r4.shape[1] // 4
    nt = S // tm
    lru_taps = p["lcw"].shape[0]
    ccm_taps = p["cdw"].shape[0]
    lpad, cpad = SUBLANES, 32
    assert lru_taps - 1 <= lpad and ccm_taps - 1 <= cpad and tm >= cpad
    row = lambda w: pl.BlockSpec((tm, w), lambda b, t: (b * nt + t, 0))
    weights = [p["lcw"], p["lcb"], p["wgate_bf"], p["bgate"], p["llam"], p["cdw"], p["cdb"], p["clg"], p["clb"],
               p["wo_bf"], p["nfg"], p["wr_bf"], p["br"]]
    return pl.pallas_call(
        functools.partial(_mix_kernel, tm=tm, n_groups=n_groups, epg=epg, lru_taps=lru_taps, ccm_taps=ccm_taps),
        grid=(B, nt),
        in_specs=[row(aw), row(4 * cw), row(D)] + [_full(w.shape) for w in weights],
        out_specs=[row(D), row(D), row(LANES), _full((SUBLANES, LANES)),
                   pl.BlockSpec((None, 1, cw), lambda b, t: (b, 0, 0)),
                   pl.BlockSpec((None, cpad, cw), lambda b, t: (b, 0, 0))],
        out_shape=[jax.ShapeDtypeStruct((T, D), F32), jax.ShapeDtypeStruct((T, D), F32),
                   jax.ShapeDtypeStruct((T, LANES), F32), jax.ShapeDtypeStruct((SUBLANES, LANES), F32),
                   jax.ShapeDtypeStruct((B, 1, cw), F32), jax.ShapeDtypeStruct((B, cpad, cw), F32)],
        scratch_shapes=[pltpu.VMEM((lpad + tm, cw), F32), pltpu.VMEM((cpad + tm, cw), F32),
                        pltpu.VMEM((tm, cw), F32), pltpu.VMEM((tm, cw), F32), pltpu.VMEM((tm, cw), F32),
                        pltpu.VMEM((SUBLANES, cw), F32), pltpu.VMEM((SUBLANES, LANES), F32)],
        compiler_params=_cparams(("arbitrary", "arbitrary")),
        name="mix",
    )(att, r4, x, *weights)


def _scatter_kernel(dest_ref, h2_ref, xs_in_ref, xs_ref, sem, *, tt):
    del xs_in_ref

    def issue(r, c):
        for k in range(2):
            d = dest_ref[2 * r + k]
            pltpu.make_async_copy(h2_ref.at[pl.ds(r, 1), :], xs_ref.at[pl.ds(d, 1), :], sem).start()
        return c

    lax.fori_loop(0, tt, issue, 0)
    for _ in range(2):
        pltpu.make_async_copy(h2_ref, xs_ref.at[pl.ds(0, tt), :], sem).wait()


def _scatter(dest, h2, cap, tt):
    T, D = h2.shape
    xs0 = jnp.zeros((cap, D), F32)
    return pl.pallas_call(
        functools.partial(_scatter_kernel, tt=tt),
        grid=(T // tt,),
        in_specs=[pl.BlockSpec((2 * tt,), lambda i: (i,), memory_space=pltpu.SMEM),
                  pl.BlockSpec((tt, D), lambda i: (i, 0)),
                  pl.BlockSpec(memory_space=pl.ANY)],
        out_specs=pl.BlockSpec(memory_space=pl.ANY),
        out_shape=jax.ShapeDtypeStruct((cap, D), F32),
        scratch_shapes=[pltpu.SemaphoreType.DMA(())],
        input_output_aliases={2: 0},
        compiler_params=_cparams(("arbitrary",)),
        name="moe_scatter",
    )(dest, h2, xs0)


def _gmm_kernel(be_ref, nu_ref, xs_ref, wg_ref, wu_ref, wd_ref, o_ref, wgb, wub, wdb):
    i = pl.program_id(0)
    e = be_ref[i]
    prev = be_ref[jnp.maximum(i - 1, 0)]

    @pl.when((i == 0) | (e != prev))
    def _():
        wgb[...] = wg_ref[...].astype(BF16)
        wub[...] = wu_ref[...].astype(BF16)
        wdb[...] = wd_ref[...].astype(BF16)

    @pl.when(i < nu_ref[0])
    def _():
        xb = xs_ref[...].astype(BF16)
        g = jnp.dot(xb, wgb[...], preferred_element_type=F32)
        u = jnp.dot(xb, wub[...], preferred_element_type=F32)
        hm = (g * _sigmoid(g) * u).astype(BF16)
        o_ref[...] = jnp.dot(hm, wdb[...], preferred_element_type=F32)

    @pl.when(i >= nu_ref[0])
    def _():
        o_ref[...] = jnp.zeros(o_ref.shape, F32)


def _gmm(block_e, n_used, xs, wg, wu, wd, layer):
    cap, D = xs.shape
    de = wg.shape[3]
    nb = cap // MOE_ROWS
    gs = pltpu.PrefetchScalarGridSpec(
        num_scalar_prefetch=2,
        grid=(nb,),
        in_specs=[pl.BlockSpec((MOE_ROWS, D), lambda i, be, nu: (i, 0)),
                  pl.BlockSpec((None, None, D, de), lambda i, be, nu: (layer, be[i], 0, 0)),
                  pl.BlockSpec((None, None, D, de), lambda i, be, nu: (layer, be[i], 0, 0)),
                  pl.BlockSpec((None, None, de, D), lambda i, be, nu: (layer, be[i], 0, 0))],
        out_specs=pl.BlockSpec((MOE_ROWS, D), lambda i, be, nu: (i, 0)),
        scratch_shapes=[pltpu.VMEM((D, de), BF16), pltpu.VMEM((D, de), BF16), pltpu.VMEM((de, D), BF16)],
    )
    return pl.pallas_call(
        _gmm_kernel,
        grid_spec=gs,
        out_shape=jax.ShapeDtypeStruct((cap, D), F32),
        compiler_params=_cparams(("arbitrary",)),
        name="moe_gmm",
    )(block_e, n_used, xs, wg, wu, wd)


def _combine_kernel(dest_ref, rec_ref, x1_ref, fg_ref, yb_ref, o_ref, buf, sem, *, tt, final_norm):
    def issue(r, c):
        for k in range(2):
            d = dest_ref[2 * r + k]
            pltpu.make_async_copy(yb_ref.at[pl.ds(d, 1), :], buf.at[k, pl.ds(r, 1), :], sem).start()
        return c

    lax.fori_loop(0, tt, issue, 0)
    for k in range(2):
        pltpu.make_async_copy(yb_ref.at[pl.ds(0, tt), :], buf.at[k], sem).wait()
    g1 = rec_ref[:, 2:3]
    g2 = rec_ref[:, 3:4]
    y = x1_ref[...] + (g1 * buf[0] + g2 * buf[1])
    if final_norm:
        y = _rms(y, fg_ref[...])
    o_ref[...] = y


def _combine(dest, rec, x1, fg, yb, tt, final_norm):
    T, D = x1.shape
    return pl.pallas_call(
        functools.partial(_combine_kernel, tt=tt, final_norm=final_norm),
        grid=(T // tt,),
        in_specs=[pl.BlockSpec((2 * tt,), lambda i: (i,), memory_space=pltpu.SMEM),
                  pl.BlockSpec((tt, LANES), lambda i: (i, 0)),
                  pl.BlockSpec((tt, D), lambda i: (i, 0)),
                  _full((1, D)),
                  pl.BlockSpec(memory_space=pl.ANY)],
        out_specs=pl.BlockSpec((tt, D), lambda i: (i, 0)),
        out_shape=jax.ShapeDtypeStruct((T, D), F32),
        scratch_shapes=[pltpu.VMEM((2, tt, D), F32), pltpu.SemaphoreType.DMA(())],
        compiler_params=_cparams(("arbitrary",)),
        name="moe_combine",
    )(dest, rec, x1, fg, yb)


def _dispatch_plan(rec, cnt, n_experts, cap):
    counts = cnt[0, :n_experts].astype(I32)
    padded = (counts + MOE_ROWS - 1) // MOE_ROWS * MOE_ROWS
    pad_end = jnp.cumsum(padded)
    pad_start = pad_end - padded
    e = rec[:, 0:2].astype(I32)
    rank = rec[:, 4:6].astype(I32)
    dest = (pad_start[e] + rank).reshape(-1)
    nb = cap // MOE_ROWS
    block_e = jnp.minimum(jnp.searchsorted(pad_end, jnp.arange(nb, dtype=I32) * MOE_ROWS, side="right"),
                          n_experts - 1).astype(I32)
    n_used = (pad_end[-1:] // MOE_ROWS).astype(I32)
    return dest, block_e, n_used


def _s_inproj_kernel(x_ref, g_ref, w_ref, o_ref):
    h = _rms(x_ref[...], g_ref[...]).astype(BF16)
    o_ref[...] = jnp.dot(h, w_ref[...], preferred_element_type=F32)


def _s_inproj(x, g, w, tn):
    n, D = x.shape
    W = w.shape[1]
    return pl.pallas_call(
        _s_inproj_kernel,
        grid=(W // tn,),
        in_specs=[_full((n, D)), _full((1, D)), pl.BlockSpec((D, tn), lambda j: (0, j))],
        out_specs=pl.BlockSpec((n, tn), lambda j: (0, j)),
        out_shape=jax.ShapeDtypeStruct((n, W), F32),
        compiler_params=_cparams(("parallel",)),
        name="s_inproj",
    )(x, g, w)


def _s_attn_kernel(pt_ref, lamv_ref, g_ref, q_ref, kn_ref, vn_ref, *rest, n_heads, pg, lam_init):
    del pt_ref
    k_refs = rest[0:pg]
    v_refs = rest[pg:2 * pg]
    o_ref, s_all, m_ref, l_ref, en_ref, acc_ref = rest[2 * pg:]
    j = pl.program_id(1)
    nk = pl.num_programs(1) // 2
    nrow = 2 * n_heads
    rows = lax.broadcasted_iota(I32, (nrow, HEAD_W), 0)
    lanes = lax.broadcasted_iota(I32, (nrow, HEAD_W), 1)
    own = jnp.where(rows >= n_heads, 1, 0) == jnp.where(lanes >= HEAD_W // 2, 1, 0)
    q8 = jnp.where(own, q_ref[...], 0.0)

    qb = q8.astype(BF16)
    prow = k_refs[0].shape[0]
    cols = pg * prow
    lam = _diff_lambda(lamv_ref[...], lam_init)

    @pl.when(j < nk)
    def _():
        @pl.when(j == 0)
        def _():
            m_ref[...] = jnp.full(m_ref.shape, -jnp.inf, F32)

        s_parts = []
        for c in range(pg):
            kp = k_refs[c][...].astype(BF16)
            s_parts.append(lax.dot_general(qb, kp, (((1,), (1,)), ((), ())), preferred_element_type=F32))
        s = jnp.concatenate(s_parts, axis=1)
        r = lax.broadcasted_iota(I32, s.shape, 0)
        col = lax.broadcasted_iota(I32, s.shape, 1)
        s = jnp.where((col & (n_heads - 1)) == (r & (n_heads - 1)), s, -jnp.inf)
        s_all[:, pl.ds(pl.multiple_of(j * cols, cols), cols)] = s
        m_ref[...] = jnp.maximum(m_ref[...], jnp.max(s, axis=-1, keepdims=True))

    @pl.when(j == nk - 1)
    def _():
        sn = jnp.sum(qb.astype(F32) * kn_ref[...].astype(BF16).astype(F32), axis=-1, keepdims=True)
        m = jnp.maximum(m_ref[...], sn)

        def expo(c, l):
            off = pl.multiple_of(c * cols, cols)
            e = jnp.exp(s_all[:, pl.ds(off, cols)] - m)
            s_all[:, pl.ds(off, cols)] = e
            return l + jnp.sum(e, axis=-1, keepdims=True)

        en = jnp.exp(sn - m)
        l_ref[...] = lax.fori_loop(0, nk, expo, jnp.zeros((nrow, 1), F32)) + en
        en_ref[...] = en
        acc_ref[...] = jnp.zeros(acc_ref.shape, F32)

    def weights(e):
        pn = e / l_ref[...]
        return (pn[0:n_heads, :] - lam * pn[n_heads:nrow, :]).astype(BF16)

    @pl.when(j >= nk)
    def _():
        w = weights(s_all[:, pl.ds(pl.multiple_of((j - nk) * cols, cols), cols)])
        pv = jnp.zeros((n_heads, HEAD_W), F32)
        for c in range(pg):
            pv += jnp.dot(w[:, c * prow:(c + 1) * prow], v_refs[c][...].astype(BF16), preferred_element_type=F32)
        acc_ref[...] += pv

    @pl.when(j == 2 * nk - 1)
    def _():
        wn = weights(en_ref[...]).astype(F32)
        o = acc_ref[...] + wn * vn_ref[0:n_heads, :].astype(BF16).astype(F32)
        o_ref[...] = _rms(o, g_ref[...]) * (1.0 - lam_init)


def _s_attn(page_table, lamv, g, q8, kn8, vn8, ck, cv, layer, n_heads, lam_init, pg):
    DB, n_pages = page_table.shape
    nrow = 2 * n_heads
    prow = ck.shape[2]

    nk = n_pages // pg

    def k_spec(c):
        return pl.BlockSpec((None, None, prow, HEAD_W),
                            lambda b, j, pt: (layer, pt[b, jnp.minimum(j, nk - 1) * pg + c], 0, 0))

    def v_spec(c):
        return pl.BlockSpec((None, None, prow, HEAD_W),
                            lambda b, j, pt: (layer, pt[b, jnp.maximum(j - nk, 0) * pg + c], 0, 0))

    tok = pl.BlockSpec((None, nrow, HEAD_W), lambda b, j, pt: (b, 0, 0))
    gs = pltpu.PrefetchScalarGridSpec(
        num_scalar_prefetch=1,
        grid=(DB, 2 * nk),
        in_specs=[pl.BlockSpec(lamv.shape, lambda b, j, pt: (0, 0)),
                  pl.BlockSpec((1, HEAD_W), lambda b, j, pt: (0, 0)),
                  tok, tok, tok] + [k_spec(c) for c in range(pg)] + [v_spec(c) for c in range(pg)],
        out_specs=pl.BlockSpec((None, n_heads, HEAD_W), lambda b, j, pt: (b, 0, 0)),
        scratch_shapes=[pltpu.VMEM((nrow, n_pages * prow), F32), pltpu.VMEM((nrow, 1), F32),
                        pltpu.VMEM((nrow, 1), F32), pltpu.VMEM((nrow, 1), F32), pltpu.VMEM((n_heads, HEAD_W), F32)],
    )
    return pl.pallas_call(
        functools.partial(_s_attn_kernel, n_heads=n_heads, pg=pg, lam_init=lam_init),
        grid_spec=gs,
        out_shape=jax.ShapeDtypeStruct((DB, n_heads, HEAD_W), F32),
        compiler_params=_cparams(("parallel", "arbitrary")),
        name="s_attn",
    )(page_table, lamv, g, q8, kn8, vn8, *([ck] * pg), *([cv] * pg))


def _s_mix_kernel(att_ref, p_ref, x_ref, h0_ref, lst_ref, cst_ref, lcw_ref, lcb_ref, wgate_ref, bgate_ref, llam_ref,
                  cdw_ref, cdb_ref, clg_ref, clb_ref, wo_ref, nfg_ref, wr_ref, br_ref,
                  x1_ref, h2_ref, rec_ref, hnew_ref, glu_ref, *, aw, cw, n_groups, epg):
    base = 3 * aw
    xr = p_ref[:, base:base + cw]
    xg = p_ref[:, base + cw:base + 2 * cw]
    ca = p_ref[:, base + 2 * cw:base + 3 * cw]
    cb = p_ref[:, base + 3 * cw:base + 4 * cw]
    lt = lcw_ref.shape[0]
    xc = jnp.sum(lst_ref[...] * lcw_ref[0:lt - 1, :][None], axis=1) + lcw_ref[lt - 1:lt, :] * xr + lcb_ref[...]
    a, u = _lru_gates(xc, wgate_ref[...], bgate_ref[...], llam_ref[...])
    h = a * h0_ref[...] + u
    hnew_ref[...] = h
    rec = h * jax.nn.gelu(xg)
    glu = ca * _sigmoid(cb)
    glu_ref[...] = glu
    ct = cdw_ref.shape[0]
    c = jnp.sum(cst_ref[...] * cdw_ref[0:ct - 1, :][None], axis=1) + cdw_ref[ct - 1:ct, :] * glu + cdb_ref[...]
    cnv = _layer_norm_silu(c, clg_ref[...], clb_ref[...])
    def dot(a, w):
        return jnp.dot(a.astype(BF16), w, preferred_element_type=F32)

    mixed = dot(att_ref[...], wo_ref[0:aw, :]) + dot(rec, wo_ref[aw:aw + cw, :]) + dot(cnv, wo_ref[aw + cw:aw + 2 * cw, :])
    x1 = x_ref[...] + mixed
    x1_ref[...] = x1
    h2 = _rms(x1, nfg_ref[...])
    h2_ref[...] = h2
    logits = dot(h2, wr_ref[...]) + br_ref[...]
    x1e, x2e, g1, g2, lane = _route(logits, n_groups, epg)
    rec_ref[...] = _pack_lanes(lane, [x1e, x2e, g1, g2])


def _s_mix(att, pj, x, h0, lst, cst, p, aw, n_groups, epg):
    n, D = x.shape
    cw = h0.shape[1]
    ins = [att, pj, x, h0, lst, cst, p["lcw"], p["lcb"], p["wgate_bf"], p["bgate"], p["llam"], p["cdw"], p["cdb"],
           p["clg"], p["clb"], p["wo_bf"], p["nfg"], p["wr_bf"], p["br"]]
    return pl.pallas_call(
        functools.partial(_s_mix_kernel, aw=aw, cw=cw, n_groups=n_groups, epg=epg),
        grid=(1,),
        in_specs=[_full(a.shape) for a in ins],
        out_specs=[_full((n, D)), _full((n, D)), _full((n, LANES)), _full((n, cw)), _full((n, cw))],
        out_shape=[jax.ShapeDtypeStruct((n, D), F32), jax.ShapeDtypeStruct((n, D), F32),
                   jax.ShapeDtypeStruct((n, LANES), F32), jax.ShapeDtypeStruct((n, cw), F32),
                   jax.ShapeDtypeStruct((n, cw), F32)],
        compiler_params=_cparams(("arbitrary",)),
        name="s_mix",
    )(*ins)


def _s_moe_kernel(h2_ref, rec_ref, x1_ref, fg_ref, wg_ref, wu_ref, wd_ref, o_ref, acc, *, final_norm):
    e = pl.program_id(0)

    @pl.when(e == 0)
    def _():
        acc[...] = x1_ref[...]

    ef = e.astype(F32)
    rec = rec_ref[...]
    gate = (jnp.where(rec[:, 0:1] == ef, rec[:, 2:3], 0.0) + jnp.where(rec[:, 1:2] == ef, rec[:, 3:4], 0.0))
    hb = h2_ref[...].astype(BF16)
    g = jnp.dot(hb, wg_ref[...].astype(BF16), preferred_element_type=F32)
    u = jnp.dot(hb, wu_ref[...].astype(BF16), preferred_element_type=F32)
    hm = (g * _sigmoid(g) * u).astype(BF16)
    y = jnp.dot(hm, wd_ref[...].astype(BF16), preferred_element_type=F32)
    acc[...] += jnp.where(gate != 0.0, gate * y, 0.0)

    @pl.when(e == pl.num_programs(0) - 1)
    def _():
        out = acc[...]
        if final_norm:
            out = _rms(out, fg_ref[...])
        o_ref[...] = out


def _s_moe(h2, rec, x1, fg, wg, wu, wd, layer, final_norm):
    n, D = x1.shape
    _, E, _, de = wg.shape
    return pl.pallas_call(
        functools.partial(_s_moe_kernel, final_norm=final_norm),
        grid=(E,),
        in_specs=[_full((n, D)), _full((n, LANES)), _full((n, D)), _full((1, D)),
                  pl.BlockSpec((None, None, D, de), lambda e: (layer, e, 0, 0)),
                  pl.BlockSpec((None, None, D, de), lambda e: (layer, e, 0, 0)),
                  pl.BlockSpec((None, None, de, D), lambda e: (layer, e, 0, 0))],
        out_specs=_full((n, D)),
        out_shape=jax.ShapeDtypeStruct((n, D), F32),
        scratch_shapes=[pltpu.VMEM((n, D), F32)],
        compiler_params=_cparams(("arbitrary",)),
        name="s_moe",
    )(h2, rec, x1, fg, wg, wu, wd)


def _block_diag(w):
    nb, c, d = w.shape
    eye = jnp.eye(nb, dtype=w.dtype)
    return (eye[:, None, :, None] * w[:, :, None, :]).reshape(nb * c, nb * d)


def _layer_params(l, P):
    D = P["w_in"].shape[1]
    n_groups, _, epg = P["router_expert_w"].shape[1:]
    wgate = jnp.concatenate([_block_diag(P["lru_wa"][l]), _block_diag(P["lru_wx"][l])], axis=1)
    wr = jnp.concatenate([P["router_group_w"][l],
                          jnp.moveaxis(P["router_expert_w"][l], 0, 1).reshape(D, n_groups * epg)], axis=1)
    br = jnp.concatenate([P["router_group_b"][l], P["router_expert_b"][l].reshape(-1)])
    padc = LANES - wr.shape[1]
    wr = jnp.pad(wr, ((0, 0), (0, padc)))
    br = jnp.pad(br, (0, padc))[None, :]
    p = dict(
        lcw=P["lru_conv_w"][l], lcb=P["lru_conv_b"][l][None], wgate=wgate,
        bgate=jnp.concatenate([P["lru_ba"][l], P["lru_bx"][l]])[None], llam=P["lru_lambda"][l][None],
        cdw=P["ccm_dw_w"][l], cdb=P["ccm_dw_b"][l][None], clg=P["ccm_ln_g"][l][None], clb=P["ccm_ln_b"][l][None],
        wo=P["w_out"][l], nfg=P["norm_ffn_g"][l][None], wr=wr, br=br,
        lamv=jnp.stack([P["lam_q1"][l], P["lam_k1"][l], P["lam_q2"][l], P["lam_k2"][l]]),
        subg=P["attn_subln_g"][l][None], nmg=P["norm_mix_g"][l][None], w_in=P["w_in"][l],
    )
    p["wgate_bf"] = wgate.astype(BF16)
    p["wo_bf"] = p["wo"].astype(BF16)
    p["wr_bf"] = wr.astype(BF16)
    p["w_in_bf"] = p["w_in"].astype(BF16)
    return p


def _pick(n, pref):
    t = min(n, pref)
    while n % t:
        t //= 2
    return t


def kernel(x_prompt, x_sample, cache_k, cache_v, state_lru_h, state_lru_conv, state_ccm_conv, page_table, norm_mix_g, w_in, lam_q1, lam_k1, lam_q2, lam_k2, attn_subln_g, lru_conv_w, lru_conv_b, lru_wa, lru_ba, lru_wx, lru_bx, lru_lambda, ccm_dw_w, ccm_dw_b, ccm_ln_g, ccm_ln_b, w_out, norm_ffn_g, router_group_w, router_group_b, router_expert_w, router_expert_b, expert_w_gate, expert_w_up, expert_w_down, norm_final_g):
    P = dict(norm_mix_g=norm_mix_g, w_in=w_in, lam_q1=lam_q1, lam_k1=lam_k1, lam_q2=lam_q2, lam_k2=lam_k2,
             attn_subln_g=attn_subln_g, lru_conv_w=lru_conv_w, lru_conv_b=lru_conv_b, lru_wa=lru_wa, lru_ba=lru_ba,
             lru_wx=lru_wx, lru_bx=lru_bx, lru_lambda=lru_lambda, ccm_dw_w=ccm_dw_w, ccm_dw_b=ccm_dw_b,
             ccm_ln_g=ccm_ln_g, ccm_ln_b=ccm_ln_b, w_out=w_out, norm_ffn_g=norm_ffn_g,
             router_group_w=router_group_w, router_group_b=router_group_b, router_expert_w=router_expert_w,
             router_expert_b=router_expert_b)
    depth = w_in.shape[0]
    B, S, D = x_prompt.shape
    DB = x_sample.shape[0]
    n_heads = cache_v.shape[3]
    dv = cache_v.shape[4]
    assert dv == HEAD_W and cache_k.shape[4] == HEAD_W
    aw = n_heads * HEAD_W
    cw = lru_conv_w.shape[2]
    n_groups, _, epg = router_expert_w.shape[1:]
    n_experts = n_groups * epg
    scale = (HEAD_W // 2) ** -0.5
    fg = norm_final_g[None]
    params = [_layer_params(l, P) for l in range(depth)]
    lam_inits = [0.8 - 0.6 * math.exp(-0.3 * l) for l in range(depth)]

    T = B * S
    tm = _pick(S, 512)
    tq = _pick(S, 256)
    tmix = _pick(S, 256)
    tt = _pick(T, 256)
    cap = 2 * T + n_experts * MOE_ROWS
    x = x_prompt.reshape(T, D)
    ks, vs, hs, lcs, ccs = [], [], [], [], []
    for l in range(depth):
        p = params[l]
        qa, qb, k, v, kb, vb, r4 = _inproj(x, p["nmg"], p["w_in_bf"], aw, scale, tm)
        att = _attn(p["lamv"], p["subg"], qa, qb, kb, vb, B, S, lam_inits[l], tq)
        x1, h2, rec, cnt, hlast, cst = _mix(att, r4, x, p, B, S, tmix, n_groups, epg)
        dest, block_e, n_used = _dispatch_plan(rec, cnt, n_experts, cap)
        xs = _scatter(dest, h2, cap, tt)
        yb = _gmm(block_e, n_used, xs, expert_w_gate, expert_w_up, expert_w_down, l)
        x = _combine(dest, rec, x1, fg, yb, tt, l == depth - 1)
        ks.append(k.reshape(B, S, n_heads, HEAD_W))
        vs.append(v.reshape(B, S, n_heads, HEAD_W))
        hs.append(hlast.reshape(B, cw))
        lt = lru_conv_w.shape[1]
        lcs.append(r4.reshape(B, S, 4 * cw)[:, S - (lt - 1):, 0:cw])
        ct = ccm_dw_w.shape[1]
        ccs.append(cst[:, cst.shape[1] - (ct - 1):, :])
    y_prompt = x.reshape(B, S, D)
    prompt_out = (y_prompt, jnp.stack(ks), jnp.stack(vs), jnp.stack(hs), jnp.stack(lcs), jnp.stack(ccs))

    n_pool, page = cache_k.shape[1], cache_k.shape[2]
    ck = cache_k.reshape(depth, n_pool, page * n_heads, HEAD_W)
    cv = cache_v.reshape(depth, n_pool, page * n_heads, HEAD_W)
    pg = _pick(page_table.shape[1], 8)
    xs_ = x_sample.reshape(DB, D)
    ks, vs, hs, lcs, ccs = [], [], [], [], []
    for l in range(depth):
        p = params[l]
        pj = _s_inproj(xs_, p["nmg"], p["w_in_bf"], _pick(p["w_in_bf"].shape[1], 512))
        q4 = (pj[:, 0:aw] * scale).reshape(DB, n_heads, HEAD_W)
        k4 = pj[:, aw:2 * aw].reshape(DB, n_heads, HEAD_W)
        v4 = pj[:, 2 * aw:3 * aw].reshape(DB, n_heads, HEAD_W)
        dup = lambda a: jnp.concatenate([a, a], axis=1)
        att = _s_attn(page_table, p["lamv"], p["subg"], dup(q4), dup(k4), dup(v4), ck, cv, l, n_heads, lam_inits[l], pg)
        x1, h2, rec, hnew, glu = _s_mix(att.reshape(DB, aw), pj, xs_, state_lru_h[l], state_lru_conv[l],
                                        state_ccm_conv[l], p, aw, n_groups, epg)
        xs_ = _s_moe(h2, rec, x1, fg, expert_w_gate, expert_w_up, expert_w_down, l, l == depth - 1)
        ks.append(k4[:, None])
        vs.append(v4[:, None])
        hs.append(hnew)
        xr = pj[:, 3 * aw:3 * aw + cw]
        lcs.append(jnp.concatenate([state_lru_conv[l][:, 1:], xr[:, None]], axis=1))
        ccs.append(jnp.concatenate([state_ccm_conv[l][:, 1:], glu[:, None]], axis=1))
    y_sample = xs_.reshape(DB, 1, D)
    return (prompt_out[0], y_sample, prompt_out[1], prompt_out[2], prompt_out[3], prompt_out[4], prompt_out[5],
            jnp.stack(ks), jnp.stack(vs), jnp.stack(hs), jnp.stack(lcs), jnp.stack(ccs))
```

```python
import functools
import math

import jax
import jax.numpy as jnp
from jax import lax
from jax.experimental import pallas as pl
from jax.experimental.pallas import tpu as pltpu

F32 = jnp.float32
BF16 = jnp.bfloat16
I32 = jnp.int32
EPS = 1e-6
LRU_C = 8.0
HEAD_W = 128
LANES = 128
SUBLANES = 8
MOE_ROWS = 256
S_ATTN_PAGES = 8


def _cparams(sem, vmem_mb=48):
    return pltpu.CompilerParams(dimension_semantics=sem, vmem_limit_bytes=vmem_mb * 1024 * 1024)


def _full(shape):
    n = len(shape)
    return pl.BlockSpec(shape, lambda *_: (0,) * n)


def _rms(x, g):
    return x * lax.rsqrt(jnp.mean(x * x, axis=-1, keepdims=True) + EPS) * g


def _sigmoid(x):
    return jax.nn.sigmoid(x)


def _diff_lambda(lamv, lam_init):
    a = jnp.sum(lamv[0:1, :] * lamv[1:2, :], axis=-1, keepdims=True)
    b = jnp.sum(lamv[2:3, :] * lamv[3:4, :], axis=-1, keepdims=True)
    return jnp.exp(a) - jnp.exp(b) + lam_init


def _inproj_kernel(x_ref, g_ref, w_ref, qat_ref, qbt_ref, k_ref, v_ref, kb_ref, vt_ref, r_ref, *, scale):
    h = _rms(x_ref[...], g_ref[...]).astype(BF16)
    aw = k_ref.shape[1]

    def mm(lo, hi):
        return jnp.dot(h, w_ref[:, lo:hi], preferred_element_type=F32)

    q = mm(0, aw) * scale
    lane = lax.broadcasted_iota(I32, q.shape, 1)
    first = (lane & (HEAD_W - 1)) < (HEAD_W // 2)
    qat_ref[...] = jnp.where(first, q, 0.0).T.astype(BF16)
    qbt_ref[...] = jnp.where(first, 0.0, q).T.astype(BF16)
    k = mm(aw, 2 * aw)
    k_ref[...] = k
    kb_ref[...] = k.astype(BF16)
    v = mm(2 * aw, 3 * aw)
    v_ref[...] = v
    vt_ref[...] = v.T.astype(BF16)
    r_ref[...] = mm(3 * aw, w_ref.shape[1])


def _inproj(x, g, w_bf, aw, scale, tm, B, S):
    T, D = x.shape
    rw = w_bf.shape[1] - 3 * aw
    nt = S // tm
    row = lambda w: pl.BlockSpec((tm, w), lambda i: (i, 0))
    tr = pl.BlockSpec((None, aw, tm), lambda i: (i // nt, 0, i % nt))
    tshape = jax.ShapeDtypeStruct((B, aw, S), BF16)
    return pl.pallas_call(
        functools.partial(_inproj_kernel, scale=scale),
        grid=(T // tm,),
        in_specs=[row(D), _full((1, D)), _full(w_bf.shape)],
        out_specs=[tr, tr, row(aw), row(aw), row(aw), tr, row(rw)],
        out_shape=[tshape, tshape,
                   jax.ShapeDtypeStruct((T, aw), F32), jax.ShapeDtypeStruct((T, aw), F32),
                   jax.ShapeDtypeStruct((T, aw), BF16), tshape,
                   jax.ShapeDtypeStruct((T, rw), F32)],
        compiler_params=_cparams(("parallel",)),
        name="inproj",
    )(x, g, w_bf)


def _attn_kernel(lamv_ref, g_ref, qat_ref, qbt_ref, k_ref, vt_ref, o_ref, m_ref, l_ref, acc_ref, *, tq, tk, lam_init):
    i = pl.program_id(2)
    w = jnp.concatenate([qat_ref[...], qbt_ref[...]], axis=1)
    m_ref[...] = jnp.full(m_ref.shape, -jnp.inf, F32)
    l_ref[...] = jnp.zeros(l_ref.shape, F32)
    acc_ref[...] = jnp.zeros(acc_ref.shape, F32)

    nd = tq // tk

    def tiles(j0, masked):
        starts = [pl.multiple_of((j0 + d) * tk, tk) for d in range(nd)]
        ss = [jnp.dot(k_ref[pl.ds(st, tk), :], w, preferred_element_type=F32) for st in starts]
        for st, s in zip(starts, ss):
            if masked:
                kpos = st + lax.broadcasted_iota(I32, s.shape, 0)
                c = lax.broadcasted_iota(I32, s.shape, 1)
                qpos = i * tq + jnp.where(c >= tq, c - tq, c)
                s = jnp.where(kpos <= qpos, s, -jnp.inf)
            m_prev = m_ref[...]
            m_new = jnp.maximum(m_prev, jnp.max(s, axis=0, keepdims=True))
            alpha = jnp.exp(m_prev - m_new)
            p = jnp.exp(s - m_new)
            l_ref[...] = alpha * l_ref[...] + jnp.sum(p, axis=0, keepdims=True)
            vtj = vt_ref[:, pl.ds(st, tk)]
            acc_ref[...] = alpha * acc_ref[...] + jnp.dot(vtj, p.astype(BF16), preferred_element_type=F32)
            m_ref[...] = m_new

    def body(jj, c):
        tiles(jj * nd, False)
        return c

    lax.fori_loop(0, i, body, 0)
    tiles(i * nd, True)

    lam = _diff_lambda(lamv_ref[...], lam_init)
    on = acc_ref[...] / l_ref[...]
    o = on[:, 0:tq] - lam * on[:, tq:2 * tq]
    y = o * lax.rsqrt(jnp.mean(o * o, axis=0, keepdims=True) + EPS) * g_ref[...] * (1.0 - lam_init)
    o_ref[...] = y.T.astype(BF16)


def _attn(lamv, gcol, qat, qbt, kb, vt, lam_init, tq, tk):
    B, aw, S = qat.shape
    H = aw // HEAD_W
    nq = S // tq
    qspec = pl.BlockSpec((None, HEAD_W, tq), lambda b, h, i: (b, h, i))
    return pl.pallas_call(
        functools.partial(_attn_kernel, tq=tq, tk=tk, lam_init=lam_init),
        grid=(B, H, nq),
        in_specs=[_full(lamv.shape), _full((HEAD_W, 1)), qspec, qspec,
                  pl.BlockSpec((S, HEAD_W), lambda b, h, i: (b, h)),
                  pl.BlockSpec((None, HEAD_W, S), lambda b, h, i: (b, h, 0))],
        out_specs=pl.BlockSpec((tq, HEAD_W), lambda b, h, i: (b * nq + i, h)),
        out_shape=jax.ShapeDtypeStruct((B * S, aw), BF16),
        scratch_shapes=[pltpu.VMEM((1, 2 * tq), F32), pltpu.VMEM((1, 2 * tq), F32),
                        pltpu.VMEM((HEAD_W, 2 * tq), F32)],
        compiler_params=_cparams(("parallel", "parallel", "arbitrary")),
        name="attn",
    )(lamv, gcol, qat, qbt, kb, vt)


def _lru_gates(xc, wgate, bgate, llam):
    cw = xc.shape[-1]
    gz = jnp.dot(xc.astype(BF16), wgate, preferred_element_type=F32) + bgate
    r = _sigmoid(gz[:, 0:cw])
    ig = _sigmoid(gz[:, cw:2 * cw])
    log_a = -LRU_C * r * jax.nn.softplus(-llam)
    a = jnp.exp(log_a)
    u = jnp.sqrt(-jnp.tanh(log_a) * (a * a + 1.0)) * ig * xc
    return a, u


def _layer_norm_silu(c, g, b):
    mu = jnp.mean(c, axis=-1, keepdims=True)
    cc = c - mu
    var = jnp.mean(cc * cc, axis=-1, keepdims=True)
    y = cc * lax.rsqrt(var + EPS) * g + b
    return y * _sigmoid(y)


def _route(logits, n_groups, epg):
    lane = lax.broadcasted_iota(I32, logits.shape, 1).astype(F32)
    big = float(LANES)
    ninf = -jnp.inf
    gl = jnp.where(lane < n_groups, logits, ninf)
    gmax = jnp.max(gl, axis=-1, keepdims=True)
    gsel = jnp.min(jnp.where(gl == gmax, lane, big), axis=-1, keepdims=True)
    p_g = 1.0 / jnp.sum(jnp.exp(gl - gmax), axis=-1, keepdims=True)
    lo = n_groups + epg * gsel
    el = jnp.where((lane >= lo) & (lane < lo + epg), logits, ninf)
    v1 = jnp.max(el, axis=-1, keepdims=True)
    i1 = jnp.min(jnp.where(el == v1, lane, big), axis=-1, keepdims=True)
    el2 = jnp.where(lane == i1, ninf, el)
    v2 = jnp.max(el2, axis=-1, keepdims=True)
    i2 = jnp.min(jnp.where(el2 == v2, lane, big), axis=-1, keepdims=True)
    e2 = jnp.exp(v2 - v1)
    den = 1.0 + e2
    g1 = (1.0 / den) * p_g
    g2 = (e2 / den) * p_g
    return i1 - n_groups, i2 - n_groups, g1, g2, lane


def _pack_lanes(lane, vals):
    out = jnp.zeros(lane.shape, F32)
    for idx, v in enumerate(vals):
        out = jnp.where(lane == float(idx), v, out)
    return out


def _mix_kernel(att_ref, r_ref, x_ref, lcw_ref, lcb_ref, wgate_ref, bgate_ref, llam_ref,
                cdw_ref, cdb_ref, clg_ref, clb_ref, wo_ref, nfg_ref, wr_ref, br_ref,
                x1_ref, h2_ref, rec_ref, cnt_ref, hlast_ref, cst_ref,
                xe, ce, a_s, u_s, hs, hc, cnt_s, *, tm, n_groups, epg, lru_taps, ccm_taps):
    b = pl.program_id(0)
    t = pl.program_id(1)
    cw = xe.shape[1]
    lpad = xe.shape[0] - tm
    cpad = ce.shape[0] - tm

    @pl.when(t == 0)
    def _():
        xe[0:lpad, :] = jnp.zeros((lpad, cw), F32)
        ce[0:cpad, :] = jnp.zeros((cpad, cw), F32)
        hc[...] = jnp.zeros(hc.shape, F32)

    @pl.when((t == 0) & (b == 0))
    def _():
        cnt_s[...] = jnp.zeros(cnt_s.shape, F32)

    xg = r_ref[:, cw:2 * cw]
    xe[lpad:lpad + tm, :] = r_ref[:, 0:cw]
    xc = jnp.zeros((tm, cw), F32) + lcb_ref[...]
    for k in range(lru_taps):
        xc = xc + lcw_ref[k:k + 1, :] * xe[pl.ds(lpad - (lru_taps - 1) + k, tm), :]
    xe[0:lpad, :] = xe[tm:tm + lpad, :]
    a, u = _lru_gates(xc, wgate_ref[...], bgate_ref[...], llam_ref[...])
    rowm = lax.broadcasted_iota(I32, (tm, cw), 0) & (SUBLANES - 1)
    for s in (1, 2, 4):
        ok = rowm >= s
        u = jnp.where(ok, a * pltpu.roll(u, s, 0) + u, u)
        a = jnp.where(ok, a * pltpu.roll(a, s, 0), a)
    a_s[...] = a
    u_s[...] = u

    def grp(gi, h):
        o = pl.multiple_of(gi * SUBLANES, SUBLANES)
        h8 = a_s[pl.ds(o, SUBLANES), :] * h + u_s[pl.ds(o, SUBLANES), :]
        hs[pl.ds(o, SUBLANES), :] = h8
        return jnp.broadcast_to(h8[SUBLANES - 1:SUBLANES, :], (SUBLANES, cw))

    hfin = lax.fori_loop(0, tm // SUBLANES, grp, hc[...])
    hc[...] = hfin
    hlast_ref[...] = hfin[0:1, :]
    rec = hs[...] * jax.nn.gelu(xg)

    glu = r_ref[:, 2 * cw:3 * cw] * _sigmoid(r_ref[:, 3 * cw:4 * cw])
    ce[cpad:cpad + tm, :] = glu
    c = jnp.zeros((tm, cw), F32) + cdb_ref[...]
    for k in range(ccm_taps):
        c = c + cdw_ref[k:k + 1, :] * ce[pl.ds(cpad - (ccm_taps - 1) + k, tm), :]
    ce[0:cpad, :] = ce[tm:tm + cpad, :]
    cst_ref[...] = ce[0:cpad, :]
    cnv = _layer_norm_silu(c, clg_ref[...], clb_ref[...])

    aw = att_ref.shape[1]
    mixed = jnp.dot(att_ref[...], wo_ref[0:aw, :], preferred_element_type=F32)
    mixed += jnp.dot(rec.astype(BF16), wo_ref[aw:aw + cw, :], preferred_element_type=F32)
    mixed += jnp.dot(cnv.astype(BF16), wo_ref[aw + cw:aw + 2 * cw, :], preferred_element_type=F32)
    x1 = x_ref[...] + mixed
    x1_ref[...] = x1

    h2 = _rms(x1, nfg_ref[...])
    h2_ref[...] = h2
    logits = jnp.dot(h2.astype(BF16), wr_ref[...], preferred_element_type=F32) + br_ref[...]
    x1e, x2e, g1, g2, lane = _route(logits, n_groups, epg)
    oh0 = (lane == x1e).astype(F32)
    oh1 = (lane == x2e).astype(F32)
    both = oh0 + oh1
    rr = lax.broadcasted_iota(I32, (tm, tm), 0)
    cc = lax.broadcasted_iota(I32, (tm, tm), 1)
    tril = (cc < rr).astype(BF16)
    before = jnp.dot(tril, both.astype(BF16), preferred_element_type=F32) + cnt_s[0:1, :]
    rank0 = jnp.sum(oh0 * before, axis=-1, keepdims=True)
    rank1 = jnp.sum(oh1 * (before + oh0), axis=-1, keepdims=True)
    cnt_new = cnt_s[0:1, :] + jnp.sum(both, axis=0, keepdims=True)
    cnt_s[...] = jnp.broadcast_to(cnt_new, cnt_s.shape)
    cnt_ref[...] = cnt_s[...]
    rec_ref[...] = _pack_lanes(lane, [x1e, x2e, g1, g2, rank0, rank1])


def _mix(att, r4, x, p, B, S, tm, n_groups, epg):
    T, D = x.shape
    aw = att.shape[1]
    cw = r4.shape[1] // 4
    nt = S // tm
    lru_taps = p["lcw"].shape[0]
    ccm_taps = p["cdw"].shape[0]
    lpad, cpad = SUBLANES, 32
    assert lru_taps - 1 <= lpad and ccm_taps - 1 <= cpad and tm >= cpad
    row = lambda w: pl.BlockSpec((tm, w), lambda b, t: (b * nt + t, 0))
    weights = [p["lcw"], p["lcb"], p["wgate_bf"], p["bgate"], p["llam"], p["cdw"], p["cdb"], p["clg"], p["clb"],
               p["wo_bf"], p["nfg"], p["wr_bf"], p["br"]]
    return pl.pallas_call(
        functools.partial(_mix_kernel, tm=tm, n_groups=n_groups, epg=epg, lru_taps=lru_taps, ccm_taps=ccm_taps),
        grid=(B, nt),
        in_specs=[row(aw), row(4 * cw), row(D)] + [_full(w.shape) for w in weights],
        out_specs=[row(D), row(D), row(LANES), _full((SUBLANES, LANES)),
                   pl.BlockSpec((None, 1, cw), lambda b, t: (b, 0, 0)),
                   pl.BlockSpec((None, cpad, cw), lambda b, t: (b, 0, 0))],
        out_shape=[jax.ShapeDtypeStruct((T, D), F32), jax.ShapeDtypeStruct((T, D), F32),
                   jax.ShapeDtypeStruct((T, LANES), F32), jax.ShapeDtypeStruct((SUBLANES, LANES), F32),
                   jax.ShapeDtypeStruct((B, 1, cw), F32), jax.ShapeDtypeStruct((B, cpad, cw), F32)],
        scratch_shapes=[pltpu.VMEM((lpad + tm, cw), F32), pltpu.VMEM((cpad + tm, cw), F32),
                        pltpu.VMEM((tm, cw), F32), pltpu.VMEM((tm, cw), F32), pltpu.VMEM((tm, cw), F32),
                        pltpu.VMEM((SUBLANES, cw), F32), pltpu.VMEM((SUBLANES, LANES), F32)],
        compiler_params=_cparams(("arbitrary", "arbitrary")),
        name="mix",
    )(att, r4, x, *weights)


def _scatter_kernel(dest_ref, h2_ref, xs_in_ref, xs_ref, sem, *, tt):
    del xs_in_ref

    def issue(r, c):
        for k in range(2):
            d = dest_ref[2 * r + k]
            pltpu.make_async_copy(h2_ref.at[pl.ds(r, 1), :], xs_ref.at[pl.ds(d, 1), :], sem).start()
        return c

    lax.fori_loop(0, tt, issue, 0)
    for _ in range(2):
        pltpu.make_async_copy(h2_ref, xs_ref.at[pl.ds(0, tt), :], sem).wait()


def _scatter(dest, h2, cap, tt):
    T, D = h2.shape
    xs0 = jnp.zeros((cap, D), F32)
    return pl.pallas_call(
        functools.partial(_scatter_kernel, tt=tt),
        grid=(T // tt,),
        in_specs=[pl.BlockSpec((2 * tt,), lambda i: (i,), memory_space=pltpu.SMEM),
                  pl.BlockSpec((tt, D), lambda i: (i, 0)),
                  pl.BlockSpec(memory_space=pl.ANY)],
        out_specs=pl.BlockSpec(memory_space=pl.ANY),
        out_shape=jax.ShapeDtypeStruct((cap, D), F32),
        scratch_shapes=[pltpu.SemaphoreType.DMA(())],
        input_output_aliases={2: 0},
        compiler_params=_cparams(("arbitrary",)),
        name="moe_scatter",
    )(dest, h2, xs0)


def _gmm_kernel(be_ref, nu_ref, xs_ref, wg_ref, wu_ref, wd_ref, o_ref, wgb, wub, wdb):
    i = pl.program_id(0)
    e = be_ref[i]
    prev = be_ref[jnp.maximum(i - 1, 0)]

    @pl.when((i == 0) | (e != prev))
    def _():
        wgb[...] = wg_ref[...].astype(BF16)
        wub[...] = wu_ref[...].astype(BF16)
        wdb[...] = wd_ref[...].astype(BF16)

    @pl.when(i < nu_ref[0])
    def _():
        xb = xs_ref[...].astype(BF16)
        g = jnp.dot(xb, wgb[...], preferred_element_type=F32)
        u = jnp.dot(xb, wub[...], preferred_element_type=F32)
        hm = (g * _sigmoid(g) * u).astype(BF16)
        o_ref[...] = jnp.dot(hm, wdb[...], preferred_element_type=F32)

    @pl.when(i >= nu_ref[0])
    def _():
        o_ref[...] = jnp.zeros(o_ref.shape, F32)


def _gmm(block_e, n_used, xs, wg, wu, wd, layer):
    cap, D = xs.shape
    de = wg.shape[3]
    nb = cap // MOE_ROWS
    gs = pltpu.PrefetchScalarGridSpec(
        num_scalar_prefetch=2,
        grid=(nb,),
        in_specs=[pl.BlockSpec((MOE_ROWS, D), lambda i, be, nu: (i, 0)),
                  pl.BlockSpec((None, None, D, de), lambda i, be, nu: (layer, be[i], 0, 0)),
                  pl.BlockSpec((None, None, D, de), lambda i, be, nu: (layer, be[i], 0, 0)),
                  pl.BlockSpec((None, None, de, D), lambda i, be, nu: (layer, be[i], 0, 0))],
        out_specs=pl.BlockSpec((MOE_ROWS, D), lambda i, be, nu: (i, 0)),
        scratch_shapes=[pltpu.VMEM((D, de), BF16), pltpu.VMEM((D, de), BF16), pltpu.VMEM((de, D), BF16)],
    )
    return pl.pallas_call(
        _gmm_kernel,
        grid_spec=gs,
        out_shape=jax.ShapeDtypeStruct((cap, D), F32),
        compiler_params=_cparams(("arbitrary",)),
        name="moe_gmm",
    )(block_e, n_used, xs, wg, wu, wd)


def _combine_kernel(dest_ref, rec_ref, x1_ref, fg_ref, yb_ref, o_ref, buf, sem, *, tt, final_norm):
    def issue(r, c):
        for k in range(2):
            d = dest_ref[2 * r + k]
            pltpu.make_async_copy(yb_ref.at[pl.ds(d, 1), :], buf.at[k, pl.ds(r, 1), :], sem).start()
        return c

    lax.fori_loop(0, tt, issue, 0)
    for k in range(2):
        pltpu.make_async_copy(yb_ref.at[pl.ds(0, tt), :], buf.at[k], sem).wait()
    g1 = rec_ref[:, 2:3]
    g2 = rec_ref[:, 3:4]
    y = x1_ref[...] + (g1 * buf[0] + g2 * buf[1])
    if final_norm:
        y = _rms(y, fg_ref[...])
    o_ref[...] = y


def _combine(dest, rec, x1, fg, yb, tt, final_norm):
    T, D = x1.shape
    return pl.pallas_call(
        functools.partial(_combine_kernel, tt=tt, final_norm=final_norm),
        grid=(T // tt,),
        in_specs=[pl.BlockSpec((2 * tt,), lambda i: (i,), memory_space=pltpu.SMEM),
                  pl.BlockSpec((tt, LANES), lambda i: (i, 0)),
                  pl.BlockSpec((tt, D), lambda i: (i, 0)),
                  _full((1, D)),
                  pl.BlockSpec(memory_space=pl.ANY)],
        out_specs=pl.BlockSpec((tt, D), lambda i: (i, 0)),
        out_shape=jax.ShapeDtypeStruct((T, D), F32),
        scratch_shapes=[pltpu.VMEM((2, tt, D), F32), pltpu.SemaphoreType.DMA(())],
        compiler_params=_cparams(("arbitrary",)),
        name="moe_combine",
    )(dest, rec, x1, fg, yb)


def _dispatch_plan(rec, cnt, n_experts, cap):
    counts = cnt[0, :n_experts].astype(I32)
    padded = (counts + MOE_ROWS - 1) // MOE_ROWS * MOE_ROWS
    pad_end = jnp.cumsum(padded)
    pad_start = pad_end - padded
    e = rec[:, 0:2].astype(I32)
    rank = rec[:, 4:6].astype(I32)
    dest = (pad_start[e] + rank).reshape(-1)
    nb = cap // MOE_ROWS
    starts = jnp.arange(nb, dtype=I32) * MOE_ROWS
    block_e = jnp.minimum(jnp.sum((pad_end[None, :] <= starts[:, None]).astype(I32), axis=1), n_experts - 1)
    n_used = (pad_end[-1:] // MOE_ROWS).astype(I32)
    return dest, block_e, n_used


def _s_inproj_kernel(x_ref, g_ref, w_ref, o_ref):
    h = _rms(x_ref[...], g_ref[...]).astype(BF16)
    o_ref[...] = jnp.dot(h, w_ref[...], preferred_element_type=F32)


def _s_inproj(x, g, w, tn):
    n, D = x.shape
    W = w.shape[1]
    return pl.pallas_call(
        _s_inproj_kernel,
        grid=(W // tn,),
        in_specs=[_full((n, D)), _full((1, D)), pl.BlockSpec((D, tn), lambda j: (0, j))],
        out_specs=pl.BlockSpec((n, tn), lambda j: (0, j)),
        out_shape=jax.ShapeDtypeStruct((n, W), F32),
        compiler_params=_cparams(("parallel",)),
        name="s_inproj",
    )(x, g, w)


def _s_attn_kernel(pt_ref, lamv_ref, g_ref, q_ref, kn_ref, vn_ref, *rest, n_heads, pg, lam_init):
    del pt_ref
    k_refs = rest[0:pg]
    v_refs = rest[pg:2 * pg]
    o_ref, s_all, m_ref, l_ref, en_ref, acc_ref = rest[2 * pg:]
    j = pl.program_id(1)
    nk = pl.num_programs(1) // 2
    nrow = 2 * n_heads
    rows = lax.broadcasted_iota(I32, (nrow, HEAD_W), 0)
    lanes = lax.broadcasted_iota(I32, (nrow, HEAD_W), 1)
    own = jnp.where(rows >= n_heads, 1, 0) == jnp.where(lanes >= HEAD_W // 2, 1, 0)
    q8 = jnp.where(own, q_ref[...], 0.0)

    qb = q8.astype(BF16)
    prow = k_refs[0].shape[0]
    cols = pg * prow
    lam = _diff_lambda(lamv_ref[...], lam_init)

    @pl.when(j < nk)
    def _():
        @pl.when(j == 0)
        def _():
            m_ref[...] = jnp.full(m_ref.shape, -jnp.inf, F32)

        s_parts = []
        for c in range(pg):
            kp = k_refs[c][...].astype(BF16)
            s_parts.append(lax.dot_general(qb, kp, (((1,), (1,)), ((), ())), preferred_element_type=F32))
        s = jnp.concatenate(s_parts, axis=1)
        r = lax.broadcasted_iota(I32, s.shape, 0)
        col = lax.broadcasted_iota(I32, s.shape, 1)
        s = jnp.where((col & (n_heads - 1)) == (r & (n_heads - 1)), s, -jnp.inf)
        s_all[:, pl.ds(pl.multiple_of(j * cols, cols), cols)] = s
        m_ref[...] = jnp.maximum(m_ref[...], jnp.max(s, axis=-1, keepdims=True))

    @pl.when(j == nk - 1)
    def _():
        sn = jnp.sum(qb.astype(F32) * kn_ref[...].astype(BF16).astype(F32), axis=-1, keepdims=True)
        m = jnp.maximum(m_ref[...], sn)

        def expo(c, l):
            off = pl.multiple_of(c * cols, cols)
            e = jnp.exp(s_all[:, pl.ds(off, cols)] - m)
            s_all[:, pl.ds(off, cols)] = e
            return l + jnp.sum(e, axis=-1, keepdims=True)

        en = jnp.exp(sn - m)
        l_ref[...] = lax.fori_loop(0, nk, expo, jnp.zeros((nrow, 1), F32)) + en
        en_ref[...] = en
        acc_ref[...] = jnp.zeros(acc_ref.shape, F32)

    def weights(e):
        pn = e / l_ref[...]
        return (pn[0:n_heads, :] - lam * pn[n_heads:nrow, :]).astype(BF16)

    @pl.when(j >= nk)
    def _():
        w = weights(s_all[:, pl.ds(pl.multiple_of((j - nk) * cols, cols), cols)])
        pv = jnp.zeros((n_heads, HEAD_W), F32)
        for c in range(pg):
            pv += jnp.dot(w[:, c * prow:(c + 1) * prow], v_refs[c][...].astype(BF16), preferred_element_type=F32)
        acc_ref[...] += pv

    @pl.when(j == 2 * nk - 1)
    def _():
        wn = weights(en_ref[...]).astype(F32)
        o = acc_ref[...] + wn * vn_ref[0:n_heads, :].astype(BF16).astype(F32)
        o_ref[...] = _rms(o, g_ref[...]) * (1.0 - lam_init)


def _s_attn(page_table, lamv, g, q8, kn8, vn8, ck, cv, layer, n_heads, lam_init, pg):
    DB, n_pages = page_table.shape
    nrow = 2 * n_heads
    prow = ck.shape[2]

    nk = n_pages // pg

    def k_spec(c):
        return pl.BlockSpec((None, None, prow, HEAD_W),
                            lambda b, j, pt: (layer, pt[b, jnp.minimum(j, nk - 1) * pg + c], 0, 0))

    def v_spec(c):
        return pl.BlockSpec((None, None, prow, HEAD_W),
                            lambda b, j, pt: (layer, pt[b, jnp.maximum(j - nk, 0) * pg + c], 0, 0))

    tok = pl.BlockSpec((None, nrow, HEAD_W), lambda b, j, pt: (b, 0, 0))
    gs = pltpu.PrefetchScalarGridSpec(
        num_scalar_prefetch=1,
        grid=(DB, 2 * nk),
        in_specs=[pl.BlockSpec(lamv.shape, lambda b, j, pt: (0, 0)),
                  pl.BlockSpec((1, HEAD_W), lambda b, j, pt: (0, 0)),
                  tok, tok, tok] + [k_spec(c) for c in range(pg)] + [v_spec(c) for c in range(pg)],
        out_specs=pl.BlockSpec((None, n_heads, HEAD_W), lambda b, j, pt: (b, 0, 0)),
        scratch_shapes=[pltpu.VMEM((nrow, n_pages * prow), F32), pltpu.VMEM((nrow, 1), F32),
                        pltpu.VMEM((nrow, 1), F32), pltpu.VMEM((nrow, 1), F32), pltpu.VMEM((n_heads, HEAD_W), F32)],
    )
    return pl.pallas_call(
        functools.partial(_s_attn_kernel, n_heads=n_heads, pg=pg, lam_init=lam_init),
        grid_spec=gs,
        out_shape=jax.ShapeDtypeStruct((DB, n_heads, HEAD_W), F32),
        compiler_params=_cparams(("parallel", "arbitrary")),
        name="s_attn",
    )(page_table, lamv, g, q8, kn8, vn8, *([ck] * pg), *([cv] * pg))


def _s_mix_kernel(att_ref, p_ref, x_ref, h0_ref, lst_ref, cst_ref, lcw_ref, lcb_ref, wgate_ref, bgate_ref, llam_ref,
                  cdw_ref, cdb_ref, clg_ref, clb_ref, wo_ref, nfg_ref, wr_ref, br_ref,
                  x1_ref, h2_ref, rec_ref, hnew_ref, glu_ref, *, aw, cw, n_groups, epg):
    base = 3 * aw
    xr = p_ref[:, base:base + cw]
    xg = p_ref[:, base + cw:base + 2 * cw]
    ca = p_ref[:, base + 2 * cw:base + 3 * cw]
    cb = p_ref[:, base + 3 * cw:base + 4 * cw]
    lt = lcw_ref.shape[0]
    xc = jnp.sum(lst_ref[...] * lcw_ref[0:lt - 1, :][None], axis=1) + lcw_ref[lt - 1:lt, :] * xr + lcb_ref[...]
    a, u = _lru_gates(xc, wgate_ref[...], bgate_ref[...], llam_ref[...])
    h = a * h0_ref[...] + u
    hnew_ref[...] = h
    rec = h * jax.nn.gelu(xg)
    glu = ca * _sigmoid(cb)
    glu_ref[...] = glu
    ct = cdw_ref.shape[0]
    c = jnp.sum(cst_ref[...] * cdw_ref[0:ct - 1, :][None], axis=1) + cdw_ref[ct - 1:ct, :] * glu + cdb_ref[...]
    cnv = _layer_norm_silu(c, clg_ref[...], clb_ref[...])
    def dot(a, w):
        return jnp.dot(a.astype(BF16), w, preferred_element_type=F32)

    mixed = dot(att_ref[...], wo_ref[0:aw, :]) + dot(rec, wo_ref[aw:aw + cw, :]) + dot(cnv, wo_ref[aw + cw:aw + 2 * cw, :])
    x1 = x_ref[...] + mixed
    x1_ref[...] = x1
    h2 = _rms(x1, nfg_ref[...])
    h2_ref[...] = h2
    logits = dot(h2, wr_ref[...]) + br_ref[...]
    x1e, x2e, g1, g2, lane = _route(logits, n_groups, epg)
    rec_ref[...] = _pack_lanes(lane, [x1e, x2e, g1, g2])


def _s_mix(att, pj, x, h0, lst, cst, p, aw, n_groups, epg):
    n, D = x.shape
    cw = h0.shape[1]
    ins = [att, pj, x, h0, lst, cst, p["lcw"], p["lcb"], p["wgate_bf"], p["bgate"], p["llam"], p["cdw"], p["cdb"],
           p["clg"], p["clb"], p["wo_bf"], p["nfg"], p["wr_bf"], p["br"]]
    return pl.pallas_call(
        functools.partial(_s_mix_kernel, aw=aw, cw=cw, n_groups=n_groups, epg=epg),
        grid=(1,),
        in_specs=[_full(a.shape) for a in ins],
        out_specs=[_full((n, D)), _full((n, D)), _full((n, LANES)), _full((n, cw)), _full((n, cw))],
        out_shape=[jax.ShapeDtypeStruct((n, D), F32), jax.ShapeDtypeStruct((n, D), F32),
                   jax.ShapeDtypeStruct((n, LANES), F32), jax.ShapeDtypeStruct((n, cw), F32),
                   jax.ShapeDtypeStruct((n, cw), F32)],
        compiler_params=_cparams(("arbitrary",)),
        name="s_mix",
    )(*ins)


def _s_moe_kernel(h2_ref, rec_ref, x1_ref, fg_ref, wg_ref, wu_ref, wd_ref, o_ref, acc, *, final_norm):
    e = pl.program_id(0)

    @pl.when(e == 0)
    def _():
        acc[...] = x1_ref[...]

    ef = e.astype(F32)
    rec = rec_ref[...]
    gate = (jnp.where(rec[:, 0:1] == ef, rec[:, 2:3], 0.0) + jnp.where(rec[:, 1:2] == ef, rec[:, 3:4], 0.0))
    hb = h2_ref[...].astype(BF16)
    g = jnp.dot(hb, wg_ref[...].astype(BF16), preferred_element_type=F32)
    u = jnp.dot(hb, wu_ref[...].astype(BF16), preferred_element_type=F32)
    hm = (g * _sigmoid(g) * u).astype(BF16)
    y = jnp.dot(hm, wd_ref[...].astype(BF16), preferred_element_type=F32)
    acc[...] += jnp.where(gate != 0.0, gate * y, 0.0)

    @pl.when(e == pl.num_programs(0) - 1)
    def _():
        out = acc[...]
        if final_norm:
            out = _rms(out, fg_ref[...])
        o_ref[...] = out


def _s_moe(h2, rec, x1, fg, wg, wu, wd, layer, final_norm):
    n, D = x1.shape
    _, E, _, de = wg.shape
    return pl.pallas_call(
        functools.partial(_s_moe_kernel, final_norm=final_norm),
        grid=(E,),
        in_specs=[_full((n, D)), _full((n, LANES)), _full((n, D)), _full((1, D)),
                  pl.BlockSpec((None, None, D, de), lambda e: (layer, e, 0, 0)),
                  pl.BlockSpec((None, None, D, de), lambda e: (layer, e, 0, 0)),
                  pl.BlockSpec((None, None, de, D), lambda e: (layer, e, 0, 0))],
        out_specs=_full((n, D)),
        out_shape=jax.ShapeDtypeStruct((n, D), F32),
        scratch_shapes=[pltpu.VMEM((n, D), F32)],
        compiler_params=_cparams(("arbitrary",)),
        name="s_moe",
    )(h2, rec, x1, fg, wg, wu, wd)


def _block_diag(w):
    nb, c, d = w.shape
    eye = jnp.eye(nb, dtype=w.dtype)
    return (eye[:, None, :, None] * w[:, :, None, :]).reshape(nb * c, nb * d)


def _layer_params(l, P):
    D = P["w_in"].shape[1]
    n_groups, _, epg = P["router_expert_w"].shape[1:]
    wgate = jnp.concatenate([_block_diag(P["lru_wa"][l]), _block_diag(P["lru_wx"][l])], axis=1)
    wr = jnp.concatenate([P["router_group_w"][l],
                          jnp.moveaxis(P["router_expert_w"][l], 0, 1).reshape(D, n_groups * epg)], axis=1)
    br = jnp.concatenate([P["router_group_b"][l], P["router_expert_b"][l].reshape(-1)])
    padc = LANES - wr.shape[1]
    wr = jnp.pad(wr, ((0, 0), (0, padc)))
    br = jnp.pad(br, (0, padc))[None, :]
    p = dict(
        lcw=P["lru_conv_w"][l], lcb=P["lru_conv_b"][l][None], wgate=wgate,
        bgate=jnp.concatenate([P["lru_ba"][l], P["lru_bx"][l]])[None], llam=P["lru_lambda"][l][None],
        cdw=P["ccm_dw_w"][l], cdb=P["ccm_dw_b"][l][None], clg=P["ccm_ln_g"][l][None], clb=P["ccm_ln_b"][l][None],
        wo=P["w_out"][l], nfg=P["norm_ffn_g"][l][None], wr=wr, br=br,
        lamv=jnp.stack([P["lam_q1"][l], P["lam_k1"][l], P["lam_q2"][l], P["lam_k2"][l]]),
        subg=P["attn_subln_g"][l][None], nmg=P["norm_mix_g"][l][None], w_in=P["w_in"][l],
    )
    p["wgate_bf"] = wgate.astype(BF16)
    p["wo_bf"] = p["wo"].astype(BF16)
    p["wr_bf"] = wr.astype(BF16)
    p["w_in_bf"] = p["w_in"].astype(BF16)
    return p


def _pick(n, pref):
    t = min(n, pref)
    while n % t:
        t //= 2
    return t


def kernel(x_prompt, x_sample, cache_k, cache_v, state_lru_h, state_lru_conv, state_ccm_conv, page_table, norm_mix_g, w_in, lam_q1, lam_k1, lam_q2, lam_k2, attn_subln_g, lru_conv_w, lru_conv_b, lru_wa, lru_ba, lru_wx, lru_bx, lru_lambda, ccm_dw_w, ccm_dw_b, ccm_ln_g, ccm_ln_b, w_out, norm_ffn_g, router_group_w, router_group_b, router_expert_w, router_expert_b, expert_w_gate, expert_w_up, expert_w_down, norm_final_g):
    P = dict(norm_mix_g=norm_mix_g, w_in=w_in, lam_q1=lam_q1, lam_k1=lam_k1, lam_q2=lam_q2, lam_k2=lam_k2,
             attn_subln_g=attn_subln_g, lru_conv_w=lru_conv_w, lru_conv_b=lru_conv_b, lru_wa=lru_wa, lru_ba=lru_ba,
             lru_wx=lru_wx, lru_bx=lru_bx, lru_lambda=lru_lambda, ccm_dw_w=ccm_dw_w, ccm_dw_b=ccm_dw_b,
             ccm_ln_g=ccm_ln_g, ccm_ln_b=ccm_ln_b, w_out=w_out, norm_ffn_g=norm_ffn_g,
             router_group_w=router_group_w, router_group_b=router_group_b, router_expert_w=router_expert_w,
             router_expert_b=router_expert_b)
    depth = w_in.shape[0]
    B, S, D = x_prompt.shape
    DB = x_sample.shape[0]
    n_heads = cache_v.shape[3]
    dv = cache_v.shape[4]
    assert dv == HEAD_W and cache_k.shape[4] == HEAD_W
    aw = n_heads * HEAD_W
    cw = lru_conv_w.shape[2]
    n_groups, _, epg = router_expert_w.shape[1:]
    n_experts = n_groups * epg
    scale = (HEAD_W // 2) ** -0.5
    fg = norm_final_g[None]
    params = [_layer_params(l, P) for l in range(depth)]
    lam_inits = [0.8 - 0.6 * math.exp(-0.3 * l) for l in range(depth)]

    T = B * S
    tm = _pick(S, 512)
    tq = _pick(S, 1024)
    tk = _pick(tq, 256)
    tmix = _pick(S, 256)
    tt = _pick(T, 256)
    cap = 2 * T + n_experts * MOE_ROWS
    x = x_prompt.reshape(T, D)
    ks, vs, hs, lcs, ccs = [], [], [], [], []
    for l in range(depth):
        p = params[l]
        qat, qbt, k, v, kb, vt, r4 = _inproj(x, p["nmg"], p["w_in_bf"], aw, scale, tm, B, S)
        att = _attn(p["lamv"], p["subg"].reshape(HEAD_W, 1), qat, qbt, kb, vt, lam_inits[l], tq, tk)
        x1, h2, rec, cnt, hlast, cst = _mix(att, r4, x, p, B, S, tmix, n_groups, epg)
        dest, block_e, n_used = _dispatch_plan(rec, cnt, n_experts, cap)
        xs = _scatter(dest, h2, cap, tt)
        yb = _gmm(block_e, n_used, xs, expert_w_gate, expert_w_up, expert_w_down, l)
        x = _combine(dest, rec, x1, fg, yb, tt, l == depth - 1)
        ks.append(k.reshape(B, S, n_heads, HEAD_W))
        vs.append(v.reshape(B, S, n_heads, HEAD_W))
        hs.append(hlast.reshape(B, cw))
        lt = lru_conv_w.shape[1]
        lcs.append(r4.reshape(B, S, 4 * cw)[:, S - (lt - 1):, 0:cw])
        ct = ccm_dw_w.shape[1]
        ccs.append(cst[:, cst.shape[1] - (ct - 1):, :])
    y_prompt = x.reshape(B, S, D)
    prompt_out = (y_prompt, jnp.stack(ks), jnp.stack(vs), jnp.stack(hs), jnp.stack(lcs), jnp.stack(ccs))

    n_pool, page = cache_k.shape[1], cache_k.shape[2]
    ck = cache_k.reshape(depth, n_pool, page * n_heads, HEAD_W)
    cv = cache_v.reshape(depth, n_pool, page * n_heads, HEAD_W)
    pg = _pick(page_table.shape[1], S_ATTN_PAGES)
    xs_ = x_sample.reshape(DB, D)
    ks, vs, hs, lcs, ccs = [], [], [], [], []
    for l in range(depth):
        p = params[l]
        pj = _s_inproj(xs_, p["nmg"], p["w_in_bf"], _pick(p["w_in_bf"].shape[1], 512))
        q4 = (pj[:, 0:aw] * scale).reshape(DB, n_heads, HEAD_W)
        k4 = pj[:, aw:2 * aw].reshape(DB, n_heads, HEAD_W)
        v4 = pj[:, 2 * aw:3 * aw].reshape(DB, n_heads, HEAD_W)
        dup = lambda a: jnp.concatenate([a, a], axis=1)
        att = _s_attn(page_table, p["lamv"], p["subg"], dup(q4), dup(k4), dup(v4), ck, cv, l, n_heads, lam_inits[l], pg)
        x1, h2, rec, hnew, glu = _s_mix(att.reshape(DB, aw), pj, xs_, state_lru_h[l], state_lru_conv[l],
                                        state_ccm_conv[l], p, aw, n_groups, epg)
        xs_ = _s_moe(h2, rec, x1, fg, expert_w_gate, expert_w_up, expert_w_down, l, l == depth - 1)
        ks.append(k4[:, None])
        vs.append(v4[:, None])
        hs.append(hnew)
        xr = pj[:, 3 * aw:3 * aw + cw]
        lcs.append(jnp.concatenate([state_lru_conv[l][:, 1:], xr[:, None]], axis=1))
        ccs.append(jnp.concatenate([state_ccm_conv[l][:, 1:], glu[:, None]], axis=1))
    y_sample = xs_.reshape(DB, 1, D)
    return (prompt_out[0], y_sample, prompt_out[1], prompt_out[2], prompt_out[3], prompt_out[4], prompt_out[5],
            jnp.stack(ks), jnp.stack(vs), jnp.stack(hs), jnp.stack(lcs), jnp.stack(ccs))
```

```python
import functools
import math

import jax
import jax.numpy as jnp
from jax import lax
from jax.experimental import pallas as pl
from jax.experimental.pallas import tpu as pltpu

F32 = jnp.float32
BF16 = jnp.bfloat16
I32 = jnp.int32
EPS = 1e-6
LRU_C = 8.0
HEAD_W = 128
LANES = 128
SUBLANES = 8
MOE_ROWS = 256
S_ATTN_PAGES = 16


def _cparams(sem, vmem_mb=48):
    return pltpu.CompilerParams(dimension_semantics=sem, vmem_limit_bytes=vmem_mb * 1024 * 1024)


def _full(shape):
    n = len(shape)
    return pl.BlockSpec(shape, lambda *_: (0,) * n)


def _rms(x, g):
    return x * lax.rsqrt(jnp.mean(x * x, axis=-1, keepdims=True) + EPS) * g


def _sigmoid(x):
    return jax.nn.sigmoid(x)


def _diff_lambda(lamv, lam_init):
    a = jnp.sum(lamv[0:1, :] * lamv[1:2, :], axis=-1, keepdims=True)
    b = jnp.sum(lamv[2:3, :] * lamv[3:4, :], axis=-1, keepdims=True)
    return jnp.exp(a) - jnp.exp(b) + lam_init


def _inproj_kernel(x_ref, g_ref, w_ref, qat_ref, qbt_ref, k_ref, v_ref, kb_ref, vt_ref, r_ref, *, scale):
    h = _rms(x_ref[...], g_ref[...]).astype(BF16)
    aw = kb_ref.shape[1]
    tm = kb_ref.shape[0]
    n_heads = aw // HEAD_W

    def mm(lo, hi):
        return jnp.dot(h, w_ref[:, lo:hi], preferred_element_type=F32)

    def rows_by_head(ref, val):
        for hd in range(n_heads):
            ref[pl.ds(hd, tm, stride=n_heads), :] = val[:, hd * HEAD_W:(hd + 1) * HEAD_W]

    q = mm(0, aw) * scale
    lane = lax.broadcasted_iota(I32, q.shape, 1)
    first = (lane & (HEAD_W - 1)) < (HEAD_W // 2)
    qat_ref[...] = jnp.where(first, q, 0.0).T.astype(BF16)
    qbt_ref[...] = jnp.where(first, 0.0, q).T.astype(BF16)
    k = mm(aw, 2 * aw)
    rows_by_head(k_ref, k)
    kb_ref[...] = k.astype(BF16)
    v = mm(2 * aw, 3 * aw)
    rows_by_head(v_ref, v)
    vt_ref[...] = v.T.astype(BF16)
    r_ref[...] = mm(3 * aw, w_ref.shape[1])


def _inproj(x, g, w_bf, aw, scale, tm, B, S):
    T, D = x.shape
    rw = w_bf.shape[1] - 3 * aw
    nt = S // tm
    row = lambda w: pl.BlockSpec((tm, w), lambda i: (i, 0))
    tr = pl.BlockSpec((None, aw, tm), lambda i: (i // nt, 0, i % nt))
    tshape = jax.ShapeDtypeStruct((B, aw, S), BF16)
    n_heads = aw // HEAD_W
    byhead = pl.BlockSpec((tm * n_heads, HEAD_W), lambda i: (i, 0))
    hshape = jax.ShapeDtypeStruct((T * n_heads, HEAD_W), F32)
    return pl.pallas_call(
        functools.partial(_inproj_kernel, scale=scale),
        grid=(T // tm,),
        in_specs=[row(D), _full((1, D)), _full(w_bf.shape)],
        out_specs=[tr, tr, byhead, byhead, row(aw), tr, row(rw)],
        out_shape=[tshape, tshape, hshape, hshape,
                   jax.ShapeDtypeStruct((T, aw), BF16), tshape,
                   jax.ShapeDtypeStruct((T, rw), F32)],
        compiler_params=_cparams(("parallel",)),
        name="inproj",
    )(x, g, w_bf)


def _attn_kernel(lamv_ref, g_ref, qat_ref, qbt_ref, k_ref, vt_ref, o_ref, m_ref, l_ref, acc_ref, *, tq, tk, lam_init):
    i = pl.program_id(2)
    w = jnp.concatenate([qat_ref[...], qbt_ref[...]], axis=1)
    m_ref[...] = jnp.full(m_ref.shape, -jnp.inf, F32)
    l_ref[...] = jnp.zeros(l_ref.shape, F32)
    acc_ref[...] = jnp.zeros(acc_ref.shape, F32)

    nd = tq // tk

    def tiles(j0, masked):
        starts = [pl.multiple_of((j0 + d) * tk, tk) for d in range(nd)]
        ss = [jnp.dot(k_ref[pl.ds(st, tk), :], w, preferred_element_type=F32) for st in starts]
        for st, s in zip(starts, ss):
            if masked:
                kpos = st + lax.broadcasted_iota(I32, s.shape, 0)
                c = lax.broadcasted_iota(I32, s.shape, 1)
                qpos = i * tq + jnp.where(c >= tq, c - tq, c)
                s = jnp.where(kpos <= qpos, s, -jnp.inf)
            m_prev = m_ref[...]
            m_new = jnp.maximum(m_prev, jnp.max(s, axis=0, keepdims=True))
            alpha = jnp.exp(m_prev - m_new)
            p = jnp.exp(s - m_new)
            l_ref[...] = alpha * l_ref[...] + jnp.sum(p, axis=0, keepdims=True)
            vtj = vt_ref[:, pl.ds(st, tk)]
            acc_ref[...] = alpha * acc_ref[...] + jnp.dot(vtj, p.astype(BF16), preferred_element_type=F32)
            m_ref[...] = m_new

    def body(jj, c):
        tiles(jj * nd, False)
        return c

    lax.fori_loop(0, i, body, 0)
    tiles(i * nd, True)

    lam = _diff_lambda(lamv_ref[...], lam_init)
    on = acc_ref[...] / l_ref[...]
    o = on[:, 0:tq] - lam * on[:, tq:2 * tq]
    y = o * lax.rsqrt(jnp.mean(o * o, axis=0, keepdims=True) + EPS) * g_ref[...] * (1.0 - lam_init)
    o_ref[...] = y.T.astype(BF16)


def _attn(lamv, gcol, qat, qbt, kb, vt, lam_init, tq, tk):
    B, aw, S = qat.shape
    H = aw // HEAD_W
    nq = S // tq
    qspec = pl.BlockSpec((None, HEAD_W, tq), lambda b, h, i: (b, h, i))
    return pl.pallas_call(
        functools.partial(_attn_kernel, tq=tq, tk=tk, lam_init=lam_init),
        grid=(B, H, nq),
        in_specs=[_full(lamv.shape), _full((HEAD_W, 1)), qspec, qspec,
                  pl.BlockSpec((S, HEAD_W), lambda b, h, i: (b, h)),
                  pl.BlockSpec((None, HEAD_W, S), lambda b, h, i: (b, h, 0))],
        out_specs=pl.BlockSpec((tq, HEAD_W), lambda b, h, i: (b * nq + i, h)),
        out_shape=jax.ShapeDtypeStruct((B * S, aw), BF16),
        scratch_shapes=[pltpu.VMEM((1, 2 * tq), F32), pltpu.VMEM((1, 2 * tq), F32),
                        pltpu.VMEM((HEAD_W, 2 * tq), F32)],
        compiler_params=_cparams(("parallel", "parallel", "arbitrary")),
        name="attn",
    )(lamv, gcol, qat, qbt, kb, vt)


def _lru_gates(xc, wgate, bgate, llam):
    cw = xc.shape[-1]
    gz = jnp.dot(xc.astype(BF16), wgate, preferred_element_type=F32) + bgate
    r = _sigmoid(gz[:, 0:cw])
    ig = _sigmoid(gz[:, cw:2 * cw])
    log_a = -LRU_C * r * jax.nn.softplus(-llam)
    a = jnp.exp(log_a)
    u = jnp.sqrt(-jnp.tanh(log_a) * (a * a + 1.0)) * ig * xc
    return a, u


def _layer_norm_silu(c, g, b):
    mu = jnp.mean(c, axis=-1, keepdims=True)
    cc = c - mu
    var = jnp.mean(cc * cc, axis=-1, keepdims=True)
    y = cc * lax.rsqrt(var + EPS) * g + b
    return y * _sigmoid(y)


def _route(logits, n_groups, epg):
    lane = lax.broadcasted_iota(I32, logits.shape, 1).astype(F32)
    big = float(LANES)
    ninf = -jnp.inf
    gl = jnp.where(lane < n_groups, logits, ninf)
    gmax = jnp.max(gl, axis=-1, keepdims=True)
    gsel = jnp.min(jnp.where(gl == gmax, lane, big), axis=-1, keepdims=True)
    p_g = 1.0 / jnp.sum(jnp.exp(gl - gmax), axis=-1, keepdims=True)
    lo = n_groups + epg * gsel
    el = jnp.where((lane >= lo) & (lane < lo + epg), logits, ninf)
    v1 = jnp.max(el, axis=-1, keepdims=True)
    i1 = jnp.min(jnp.where(el == v1, lane, big), axis=-1, keepdims=True)
    el2 = jnp.where(lane == i1, ninf, el)
    v2 = jnp.max(el2, axis=-1, keepdims=True)
    i2 = jnp.min(jnp.where(el2 == v2, lane, big), axis=-1, keepdims=True)
    e2 = jnp.exp(v2 - v1)
    den = 1.0 + e2
    g1 = (1.0 / den) * p_g
    g2 = (e2 / den) * p_g
    return i1 - n_groups, i2 - n_groups, g1, g2, lane


def _pack_lanes(lane, vals):
    out = jnp.zeros(lane.shape, F32)
    for idx, v in enumerate(vals):
        out = jnp.where(lane == float(idx), v, out)
    return out


def _mix_kernel(att_ref, r_ref, x_ref, lcw_ref, lcb_ref, wgate_ref, bgate_ref, llam_ref,
                cdw_ref, cdb_ref, clg_ref, clb_ref, wo_ref, nfg_ref, wr_ref, br_ref,
                x1_ref, h2_ref, rec_ref, cnt_ref, hlast_ref, cst_ref,
                xe, ce, a_s, u_s, hs, hc, cnt_s, *, tm, n_groups, epg, lru_taps, ccm_taps):
    b = pl.program_id(0)
    t = pl.program_id(1)
    cw = xe.shape[1]
    lpad = xe.shape[0] - tm
    cpad = ce.shape[0] - tm

    @pl.when(t == 0)
    def _():
        xe[0:lpad, :] = jnp.zeros((lpad, cw), F32)
        ce[0:cpad, :] = jnp.zeros((cpad, cw), F32)
        hc[...] = jnp.zeros(hc.shape, F32)

    @pl.when((t == 0) & (b == 0))
    def _():
        cnt_s[...] = jnp.zeros(cnt_s.shape, F32)

    xg = r_ref[:, cw:2 * cw]
    xe[lpad:lpad + tm, :] = r_ref[:, 0:cw]
    xc = jnp.zeros((tm, cw), F32) + lcb_ref[...]
    for k in range(lru_taps):
        xc = xc + lcw_ref[k:k + 1, :] * xe[pl.ds(lpad - (lru_taps - 1) + k, tm), :]
    xe[0:lpad, :] = xe[tm:tm + lpad, :]
    a, u = _lru_gates(xc, wgate_ref[...], bgate_ref[...], llam_ref[...])
    rowm = lax.broadcasted_iota(I32, (tm, cw), 0) & (SUBLANES - 1)
    for s in (1, 2, 4):
        ok = rowm >= s
        u = jnp.where(ok, a * pltpu.roll(u, s, 0) + u, u)
        a = jnp.where(ok, a * pltpu.roll(a, s, 0), a)
    a_s[...] = a
    u_s[...] = u

    def grp(gi, h):
        o = pl.multiple_of(gi * SUBLANES, SUBLANES)
        h8 = a_s[pl.ds(o, SUBLANES), :] * h + u_s[pl.ds(o, SUBLANES), :]
        hs[pl.ds(o, SUBLANES), :] = h8
        return jnp.broadcast_to(h8[SUBLANES - 1:SUBLANES, :], (SUBLANES, cw))

    hfin = lax.fori_loop(0, tm // SUBLANES, grp, hc[...])
    hc[...] = hfin
    hlast_ref[...] = hfin[0:1, :]
    rec = hs[...] * jax.nn.gelu(xg)

    glu = r_ref[:, 2 * cw:3 * cw] * _sigmoid(r_ref[:, 3 * cw:4 * cw])
    ce[cpad:cpad + tm, :] = glu
    c = jnp.zeros((tm, cw), F32) + cdb_ref[...]
    for k in range(ccm_taps):
        c = c + cdw_ref[k:k + 1, :] * ce[pl.ds(cpad - (ccm_taps - 1) + k, tm), :]
    ce[0:cpad, :] = ce[tm:tm + cpad, :]
    cst_ref[...] = ce[0:cpad, :]
    cnv = _layer_norm_silu(c, clg_ref[...], clb_ref[...])

    aw = att_ref.shape[1]
    mixed = jnp.dot(att_ref[...], wo_ref[0:aw, :], preferred_element_type=F32)
    mixed += jnp.dot(rec.astype(BF16), wo_ref[aw:aw + cw, :], preferred_element_type=F32)
    mixed += jnp.dot(cnv.astype(BF16), wo_ref[aw + cw:aw + 2 * cw, :], preferred_element_type=F32)
    x1 = x_ref[...] + mixed
    x1_ref[...] = x1

    h2 = _rms(x1, nfg_ref[...])
    h2_ref[...] = h2
    logits = jnp.dot(h2.astype(BF16), wr_ref[...], preferred_element_type=F32) + br_ref[...]
    x1e, x2e, g1, g2, lane = _route(logits, n_groups, epg)
    oh0 = (lane == x1e).astype(F32)
    oh1 = (lane == x2e).astype(F32)
    both = oh0 + oh1
    rr = lax.broadcasted_iota(I32, (tm, tm), 0)
    cc = lax.broadcasted_iota(I32, (tm, tm), 1)
    tril = (cc < rr).astype(BF16)
    before = jnp.dot(tril, both.astype(BF16), preferred_element_type=F32) + cnt_s[0:1, :]
    rank0 = jnp.sum(oh0 * before, axis=-1, keepdims=True)
    rank1 = jnp.sum(oh1 * (before + oh0), axis=-1, keepdims=True)
    cnt_new = cnt_s[0:1, :] + jnp.sum(both, axis=0, keepdims=True)
    cnt_s[...] = jnp.broadcast_to(cnt_new, cnt_s.shape)
    cnt_ref[...] = cnt_s[...]
    rec_ref[...] = _pack_lanes(lane, [x1e, x2e, g1, g2, rank0, rank1])


def _mix(att, r4, x, p, B, S, tm, n_groups, epg):
    T, D = x.shape
    aw = att.shape[1]
    cw = r4.shape[1] // 4
    nt = S // tm
    lru_taps = p["lcw"].shape[0]
    ccm_taps = p["cdw"].shape[0]
    lpad, cpad = SUBLANES, 32
    assert lru_taps - 1 <= lpad and ccm_taps - 1 <= cpad and tm >= cpad
    row = lambda w: pl.BlockSpec((tm, w), lambda b, t: (b * nt + t, 0))
    weights = [p["lcw"], p["lcb"], p["wgate_bf"], p["bgate"], p["llam"], p["cdw"], p["cdb"], p["clg"], p["clb"],
               p["wo_bf"], p["nfg"], p["wr_bf"], p["br"]]
    return pl.pallas_call(
        functools.partial(_mix_kernel, tm=tm, n_groups=n_groups, epg=epg, lru_taps=lru_taps, ccm_taps=ccm_taps),
        grid=(B, nt),
        in_specs=[row(aw), row(4 * cw), row(D)] + [_full(w.shape) for w in weights],
        out_specs=[row(D), row(D), row(LANES), _full((SUBLANES, LANES)),
                   pl.BlockSpec((None, 1, cw), lambda b, t: (b, 0, 0)),
                   pl.BlockSpec((None, cpad, cw), lambda b, t: (b, 0, 0))],
        out_shape=[jax.ShapeDtypeStruct((T, D), F32), jax.ShapeDtypeStruct((T, D), F32),
                   jax.ShapeDtypeStruct((T, LANES), F32), jax.ShapeDtypeStruct((SUBLANES, LANES), F32),
                   jax.ShapeDtypeStruct((B, 1, cw), F32), jax.ShapeDtypeStruct((B, cpad, cw), F32)],
        scratch_shapes=[pltpu.VMEM((lpad + tm, cw), F32), pltpu.VMEM((cpad + tm, cw), F32),
                        pltpu.VMEM((tm, cw), F32), pltpu.VMEM((tm, cw), F32), pltpu.VMEM((tm, cw), F32),
                        pltpu.VMEM((SUBLANES, cw), F32), pltpu.VMEM((SUBLANES, LANES), F32)],
        compiler_params=_cparams(("arbitrary", "arbitrary")),
        name="mix",
    )(att, r4, x, *weights)


def _scatter_kernel(dest_ref, h2_ref, xs_in_ref, xs_ref, sem, *, tt):
    del xs_in_ref

    def issue(ro, c):
        base = pl.multiple_of(ro * SUBLANES, SUBLANES)
        for ri in range(SUBLANES):
            r = base + ri
            for k in range(2):
                d = dest_ref[2 * r + k]
                pltpu.make_async_copy(h2_ref.at[pl.ds(r, 1), :], xs_ref.at[pl.ds(d, 1), :], sem).start()
        return c

    lax.fori_loop(0, tt // SUBLANES, issue, 0)
    for _ in range(2):
        pltpu.make_async_copy(h2_ref, xs_ref.at[pl.ds(0, tt), :], sem).wait()


def _scatter(dest, h2, cap, tt):
    T, D = h2.shape
    xs0 = jnp.zeros((cap, D), F32)
    return pl.pallas_call(
        functools.partial(_scatter_kernel, tt=tt),
        grid=(T // tt,),
        in_specs=[pl.BlockSpec((2 * tt,), lambda i: (i,), memory_space=pltpu.SMEM),
                  pl.BlockSpec((tt, D), lambda i: (i, 0)),
                  pl.BlockSpec(memory_space=pl.ANY)],
        out_specs=pl.BlockSpec(memory_space=pl.ANY),
        out_shape=jax.ShapeDtypeStruct((cap, D), F32),
        scratch_shapes=[pltpu.SemaphoreType.DMA(())],
        input_output_aliases={2: 0},
        compiler_params=_cparams(("arbitrary",)),
        name="moe_scatter",
    )(dest, h2, xs0)


def _gmm_kernel(be_ref, nu_ref, xs_ref, wg_ref, wu_ref, wd_ref, o_ref, wgb, wub, wdb):
    i = pl.program_id(0)
    e = be_ref[i]
    prev = be_ref[jnp.maximum(i - 1, 0)]

    @pl.when((i == 0) | (e != prev))
    def _():
        wgb[...] = wg_ref[...].astype(BF16)
        wub[...] = wu_ref[...].astype(BF16)
        wdb[...] = wd_ref[...].astype(BF16)

    @pl.when(i < nu_ref[0])
    def _():
        xb = xs_ref[...].astype(BF16)
        g = jnp.dot(xb, wgb[...], preferred_element_type=F32)
        u = jnp.dot(xb, wub[...], preferred_element_type=F32)
        hm = (g * _sigmoid(g) * u).astype(BF16)
        o_ref[...] = jnp.dot(hm, wdb[...], preferred_element_type=F32)

    @pl.when(i >= nu_ref[0])
    def _():
        o_ref[...] = jnp.zeros(o_ref.shape, F32)


def _gmm(block_e, n_used, xs, wg, wu, wd, layer):
    cap, D = xs.shape
    de = wg.shape[3]
    nb = cap // MOE_ROWS
    gs = pltpu.PrefetchScalarGridSpec(
        num_scalar_prefetch=2,
        grid=(nb,),
        in_specs=[pl.BlockSpec((MOE_ROWS, D), lambda i, be, nu: (i, 0)),
                  pl.BlockSpec((None, None, D, de), lambda i, be, nu: (layer, be[i], 0, 0)),
                  pl.BlockSpec((None, None, D, de), lambda i, be, nu: (layer, be[i], 0, 0)),
                  pl.BlockSpec((None, None, de, D), lambda i, be, nu: (layer, be[i], 0, 0))],
        out_specs=pl.BlockSpec((MOE_ROWS, D), lambda i, be, nu: (i, 0)),
        scratch_shapes=[pltpu.VMEM((D, de), BF16), pltpu.VMEM((D, de), BF16), pltpu.VMEM((de, D), BF16)],
    )
    return pl.pallas_call(
        _gmm_kernel,
        grid_spec=gs,
        out_shape=jax.ShapeDtypeStruct((cap, D), F32),
        compiler_params=_cparams(("arbitrary",)),
        name="moe_gmm",
    )(block_e, n_used, xs, wg, wu, wd)


def _combine_kernel(dest_ref, rec_ref, x1_ref, fg_ref, yb_ref, o_ref, buf, sem, *, tt, final_norm):
    def issue(ro, c):
        base = pl.multiple_of(ro * SUBLANES, SUBLANES)
        for ri in range(SUBLANES):
            r = base + ri
            for k in range(2):
                d = dest_ref[2 * r + k]
                pltpu.make_async_copy(yb_ref.at[pl.ds(d, 1), :], buf.at[k, pl.ds(r, 1), :], sem).start()
        return c

    lax.fori_loop(0, tt // SUBLANES, issue, 0)
    for k in range(2):
        pltpu.make_async_copy(yb_ref.at[pl.ds(0, tt), :], buf.at[k], sem).wait()
    g1 = rec_ref[:, 2:3]
    g2 = rec_ref[:, 3:4]
    y = x1_ref[...] + (g1 * buf[0] + g2 * buf[1])
    if final_norm:
        y = _rms(y, fg_ref[...])
    o_ref[...] = y


def _combine(dest, rec, x1, fg, yb, tt, final_norm):
    T, D = x1.shape
    return pl.pallas_call(
        functools.partial(_combine_kernel, tt=tt, final_norm=final_norm),
        grid=(T // tt,),
        in_specs=[pl.BlockSpec((2 * tt,), lambda i: (i,), memory_space=pltpu.SMEM),
                  pl.BlockSpec((tt, LANES), lambda i: (i, 0)),
                  pl.BlockSpec((tt, D), lambda i: (i, 0)),
                  _full((1, D)),
                  pl.BlockSpec(memory_space=pl.ANY)],
        out_specs=pl.BlockSpec((tt, D), lambda i: (i, 0)),
        out_shape=jax.ShapeDtypeStruct((T, D), F32),
        scratch_shapes=[pltpu.VMEM((2, tt, D), F32), pltpu.SemaphoreType.DMA(())],
        compiler_params=_cparams(("arbitrary",)),
        name="moe_combine",
    )(dest, rec, x1, fg, yb)


def _dispatch_plan(rec, cnt, n_experts, cap):
    counts = cnt[0, :n_experts].astype(I32)
    padded = (counts + MOE_ROWS - 1) // MOE_ROWS * MOE_ROWS
    pad_end = jnp.cumsum(padded)
    pad_start = pad_end - padded
    e = rec[:, 0:2].astype(I32)
    rank = rec[:, 4:6].astype(I32)
    dest = (pad_start[e] + rank).reshape(-1)
    nb = cap // MOE_ROWS
    starts = jnp.arange(nb, dtype=I32) * MOE_ROWS
    block_e = jnp.minimum(jnp.sum((pad_end[None, :] <= starts[:, None]).astype(I32), axis=1), n_experts - 1)
    n_used = (pad_end[-1:] // MOE_ROWS).astype(I32)
    return dest, block_e, n_used


def _s_inproj_kernel(x_ref, g_ref, w_ref, o_ref):
    h = _rms(x_ref[...], g_ref[...]).astype(BF16)
    o_ref[...] = jnp.dot(h, w_ref[...], preferred_element_type=F32)


def _s_inproj(x, g, w, tn):
    n, D = x.shape
    W = w.shape[1]
    return pl.pallas_call(
        _s_inproj_kernel,
        grid=(W // tn,),
        in_specs=[_full((n, D)), _full((1, D)), pl.BlockSpec((D, tn), lambda j: (0, j))],
        out_specs=pl.BlockSpec((n, tn), lambda j: (0, j)),
        out_shape=jax.ShapeDtypeStruct((n, W), F32),
        compiler_params=_cparams(("parallel",)),
        name="s_inproj",
    )(x, g, w)


def _s_attn_kernel(pt_ref, lamv_ref, g_ref, q_ref, kn_ref, vn_ref, *rest, n_heads, pg, lam_init):
    del pt_ref
    k_refs = rest[0:pg]
    v_refs = rest[pg:2 * pg]
    o_ref, s_all, m_ref, l_ref, en_ref, acc_ref = rest[2 * pg:]
    j = pl.program_id(1)
    nk = pl.num_programs(1) // 2
    nrow = 2 * n_heads
    rows = lax.broadcasted_iota(I32, (nrow, HEAD_W), 0)
    lanes = lax.broadcasted_iota(I32, (nrow, HEAD_W), 1)
    own = jnp.where(rows >= n_heads, 1, 0) == jnp.where(lanes >= HEAD_W // 2, 1, 0)
    q8 = jnp.where(own, q_ref[...], 0.0)

    qb = q8.astype(BF16)
    prow = k_refs[0].shape[0]
    cols = pg * prow
    lam = _diff_lambda(lamv_ref[...], lam_init)

    @pl.when(j < nk)
    def _():
        @pl.when(j == 0)
        def _():
            m_ref[...] = jnp.full(m_ref.shape, -jnp.inf, F32)

        s_parts = []
        for c in range(pg):
            kp = k_refs[c][...].astype(BF16)
            s_parts.append(lax.dot_general(qb, kp, (((1,), (1,)), ((), ())), preferred_element_type=F32))
        s = jnp.concatenate(s_parts, axis=1)
        r = lax.broadcasted_iota(I32, s.shape, 0)
        col = lax.broadcasted_iota(I32, s.shape, 1)
        s = jnp.where((col & (n_heads - 1)) == (r & (n_heads - 1)), s, -jnp.inf)
        s_all[:, pl.ds(pl.multiple_of(j * cols, cols), cols)] = s
        m_ref[...] = jnp.maximum(m_ref[...], jnp.max(s, axis=-1, keepdims=True))

    @pl.when(j == nk - 1)
    def _():
        sn = jnp.sum(qb.astype(F32) * kn_ref[...].astype(BF16).astype(F32), axis=-1, keepdims=True)
        m = jnp.maximum(m_ref[...], sn)

        def expo(c, l):
            off = pl.multiple_of(c * cols, cols)
            e = jnp.exp(s_all[:, pl.ds(off, cols)] - m)
            s_all[:, pl.ds(off, cols)] = e
            return l + jnp.sum(e, axis=-1, keepdims=True)

        en = jnp.exp(sn - m)
        l_ref[...] = lax.fori_loop(0, nk, expo, jnp.zeros((nrow, 1), F32)) + en
        en_ref[...] = en
        acc_ref[...] = jnp.zeros(acc_ref.shape, F32)

    def weights(e):
        pn = e / l_ref[...]
        return (pn[0:n_heads, :] - lam * pn[n_heads:nrow, :]).astype(BF16)

    @pl.when(j >= nk)
    def _():
        w = weights(s_all[:, pl.ds(pl.multiple_of((j - nk) * cols, cols), cols)])
        pv = jnp.zeros((n_heads, HEAD_W), F32)
        for c in range(pg):
            pv += jnp.dot(w[:, c * prow:(c + 1) * prow], v_refs[c][...].astype(BF16), preferred_element_type=F32)
        acc_ref[...] += pv

    @pl.when(j == 2 * nk - 1)
    def _():
        wn = weights(en_ref[...]).astype(F32)
        o = acc_ref[...] + wn * vn_ref[0:n_heads, :].astype(BF16).astype(F32)
        o_ref[...] = _rms(o, g_ref[...]) * (1.0 - lam_init)


def _s_attn(page_table, lamv, g, q8, kn8, vn8, ck, cv, layer, n_heads, lam_init, pg):
    DB, n_pages = page_table.shape
    nrow = 2 * n_heads
    prow = ck.shape[2]

    nk = n_pages // pg

    def k_spec(c):
        return pl.BlockSpec((None, None, prow, HEAD_W),
                            lambda b, j, pt: (layer, pt[b, jnp.minimum(j, nk - 1) * pg + c], 0, 0))

    def v_spec(c):
        return pl.BlockSpec((None, None, prow, HEAD_W),
                            lambda b, j, pt: (layer, pt[b, jnp.maximum(j - nk, 0) * pg + c], 0, 0))

    tok = pl.BlockSpec((None, nrow, HEAD_W), lambda b, j, pt: (b, 0, 0))
    gs = pltpu.PrefetchScalarGridSpec(
        num_scalar_prefetch=1,
        grid=(DB, 2 * nk),
        in_specs=[pl.BlockSpec(lamv.shape, lambda b, j, pt: (0, 0)),
                  pl.BlockSpec((1, HEAD_W), lambda b, j, pt: (0, 0)),
                  tok, tok, tok] + [k_spec(c) for c in range(pg)] + [v_spec(c) for c in range(pg)],
        out_specs=pl.BlockSpec((None, n_heads, HEAD_W), lambda b, j, pt: (b, 0, 0)),
        scratch_shapes=[pltpu.VMEM((nrow, n_pages * prow), F32), pltpu.VMEM((nrow, 1), F32),
                        pltpu.VMEM((nrow, 1), F32), pltpu.VMEM((nrow, 1), F32), pltpu.VMEM((n_heads, HEAD_W), F32)],
    )
    return pl.pallas_call(
        functools.partial(_s_attn_kernel, n_heads=n_heads, pg=pg, lam_init=lam_init),
        grid_spec=gs,
        out_shape=jax.ShapeDtypeStruct((DB, n_heads, HEAD_W), F32),
        compiler_params=_cparams(("parallel", "arbitrary")),
        name="s_attn",
    )(page_table, lamv, g, q8, kn8, vn8, *([ck] * pg), *([cv] * pg))


def _s_mix_kernel(att_ref, p_ref, x_ref, h0_ref, lst_ref, cst_ref, lcw_ref, lcb_ref, wgate_ref, bgate_ref, llam_ref,
                  cdw_ref, cdb_ref, clg_ref, clb_ref, wo_ref, nfg_ref, wr_ref, br_ref,
                  x1_ref, h2_ref, rec_ref, hnew_ref, glu_ref, *, aw, cw, n_groups, epg):
    base = 3 * aw
    xr = p_ref[:, base:base + cw]
    xg = p_ref[:, base + cw:base + 2 * cw]
    ca = p_ref[:, base + 2 * cw:base + 3 * cw]
    cb = p_ref[:, base + 3 * cw:base + 4 * cw]
    lt = lcw_ref.shape[0]
    xc = jnp.sum(lst_ref[...] * lcw_ref[0:lt - 1, :][None], axis=1) + lcw_ref[lt - 1:lt, :] * xr + lcb_ref[...]
    a, u = _lru_gates(xc, wgate_ref[...], bgate_ref[...], llam_ref[...])
    h = a * h0_ref[...] + u
    hnew_ref[...] = h
    rec = h * jax.nn.gelu(xg)
    glu = ca * _sigmoid(cb)
    glu_ref[...] = glu
    ct = cdw_ref.shape[0]
    c = jnp.sum(cst_ref[...] * cdw_ref[0:ct - 1, :][None], axis=1) + cdw_ref[ct - 1:ct, :] * glu + cdb_ref[...]
    cnv = _layer_norm_silu(c, clg_ref[...], clb_ref[...])
    def dot(a, w):
        return jnp.dot(a.astype(BF16), w, preferred_element_type=F32)

    mixed = dot(att_ref[...], wo_ref[0:aw, :]) + dot(rec, wo_ref[aw:aw + cw, :]) + dot(cnv, wo_ref[aw + cw:aw + 2 * cw, :])
    x1 = x_ref[...] + mixed
    x1_ref[...] = x1
    h2 = _rms(x1, nfg_ref[...])
    h2_ref[...] = h2
    logits = dot(h2, wr_ref[...]) + br_ref[...]
    x1e, x2e, g1, g2, lane = _route(logits, n_groups, epg)
    rec_ref[...] = _pack_lanes(lane, [x1e, x2e, g1, g2])


def _s_mix(att, pj, x, h0, lst, cst, p, aw, n_groups, epg):
    n, D = x.shape
    cw = h0.shape[1]
    ins = [att, pj, x, h0, lst, cst, p["lcw"], p["lcb"], p["wgate_bf"], p["bgate"], p["llam"], p["cdw"], p["cdb"],
           p["clg"], p["clb"], p["wo_bf"], p["nfg"], p["wr_bf"], p["br"]]
    return pl.pallas_call(
        functools.partial(_s_mix_kernel, aw=aw, cw=cw, n_groups=n_groups, epg=epg),
        grid=(1,),
        in_specs=[_full(a.shape) for a in ins],
        out_specs=[_full((n, D)), _full((n, D)), _full((n, LANES)), _full((n, cw)), _full((n, cw))],
        out_shape=[jax.ShapeDtypeStruct((n, D), F32), jax.ShapeDtypeStruct((n, D), F32),
                   jax.ShapeDtypeStruct((n, LANES), F32), jax.ShapeDtypeStruct((n, cw), F32),
                   jax.ShapeDtypeStruct((n, cw), F32)],
        compiler_params=_cparams(("arbitrary",)),
        name="s_mix",
    )(*ins)


def _s_moe_kernel(h2_ref, rec_ref, x1_ref, fg_ref, wg_ref, wu_ref, wd_ref, o_ref, acc, *, final_norm):
    e = pl.program_id(0)

    @pl.when(e == 0)
    def _():
        acc[...] = x1_ref[...]

    ef = e.astype(F32)
    rec = rec_ref[...]
    gate = (jnp.where(rec[:, 0:1] == ef, rec[:, 2:3], 0.0) + jnp.where(rec[:, 1:2] == ef, rec[:, 3:4], 0.0))
    hb = h2_ref[...].astype(BF16)
    g = jnp.dot(hb, wg_ref[...].astype(BF16), preferred_element_type=F32)
    u = jnp.dot(hb, wu_ref[...].astype(BF16), preferred_element_type=F32)
    hm = (g * _sigmoid(g) * u).astype(BF16)
    y = jnp.dot(hm, wd_ref[...].astype(BF16), preferred_element_type=F32)
    acc[...] += jnp.where(gate != 0.0, gate * y, 0.0)

    @pl.when(e == pl.num_programs(0) - 1)
    def _():
        out = acc[...]
        if final_norm:
            out = _rms(out, fg_ref[...])
        o_ref[...] = out


def _s_moe(h2, rec, x1, fg, wg, wu, wd, layer, final_norm):
    n, D = x1.shape
    _, E, _, de = wg.shape
    return pl.pallas_call(
        functools.partial(_s_moe_kernel, final_norm=final_norm),
        grid=(E,),
        in_specs=[_full((n, D)), _full((n, LANES)), _full((n, D)), _full((1, D)),
                  pl.BlockSpec((None, None, D, de), lambda e: (layer, e, 0, 0)),
                  pl.BlockSpec((None, None, D, de), lambda e: (layer, e, 0, 0)),
                  pl.BlockSpec((None, None, de, D), lambda e: (layer, e, 0, 0))],
        out_specs=_full((n, D)),
        out_shape=jax.ShapeDtypeStruct((n, D), F32),
        scratch_shapes=[pltpu.VMEM((n, D), F32)],
        compiler_params=_cparams(("arbitrary",)),
        name="s_moe",
    )(h2, rec, x1, fg, wg, wu, wd)


def _block_diag(w):
    nb, c, d = w.shape
    eye = jnp.eye(nb, dtype=w.dtype)
    return (eye[:, None, :, None] * w[:, :, None, :]).reshape(nb * c, nb * d)


def _layer_params(l, P):
    D = P["w_in"].shape[1]
    n_groups, _, epg = P["router_expert_w"].shape[1:]
    wgate = jnp.concatenate([_block_diag(P["lru_wa"][l]), _block_diag(P["lru_wx"][l])], axis=1)
    wr = jnp.concatenate([P["router_group_w"][l],
                          jnp.moveaxis(P["router_expert_w"][l], 0, 1).reshape(D, n_groups * epg)], axis=1)
    br = jnp.concatenate([P["router_group_b"][l], P["router_expert_b"][l].reshape(-1)])
    padc = LANES - wr.shape[1]
    wr = jnp.pad(wr, ((0, 0), (0, padc)))
    br = jnp.pad(br, (0, padc))[None, :]
    p = dict(
        lcw=P["lru_conv_w"][l], lcb=P["lru_conv_b"][l][None], wgate=wgate,
        bgate=jnp.concatenate([P["lru_ba"][l], P["lru_bx"][l]])[None], llam=P["lru_lambda"][l][None],
        cdw=P["ccm_dw_w"][l], cdb=P["ccm_dw_b"][l][None], clg=P["ccm_ln_g"][l][None], clb=P["ccm_ln_b"][l][None],
        wo=P["w_out"][l], nfg=P["norm_ffn_g"][l][None], wr=wr, br=br,
        lamv=jnp.stack([P["lam_q1"][l], P["lam_k1"][l], P["lam_q2"][l], P["lam_k2"][l]]),
        subg=P["attn_subln_g"][l][None], nmg=P["norm_mix_g"][l][None], w_in=P["w_in"][l],
    )
    p["wgate_bf"] = wgate.astype(BF16)
    p["wo_bf"] = p["wo"].astype(BF16)
    p["wr_bf"] = wr.astype(BF16)
    p["w_in_bf"] = p["w_in"].astype(BF16)
    return p


def _pick(n, pref):
    t = min(n, pref)
    while n % t:
        t //= 2
    return t


def kernel(x_prompt, x_sample, cache_k, cache_v, state_lru_h, state_lru_conv, state_ccm_conv, page_table, norm_mix_g, w_in, lam_q1, lam_k1, lam_q2, lam_k2, attn_subln_g, lru_conv_w, lru_conv_b, lru_wa, lru_ba, lru_wx, lru_bx, lru_lambda, ccm_dw_w, ccm_dw_b, ccm_ln_g, ccm_ln_b, w_out, norm_ffn_g, router_group_w, router_group_b, router_expert_w, router_expert_b, expert_w_gate, expert_w_up, expert_w_down, norm_final_g):
    P = dict(norm_mix_g=norm_mix_g, w_in=w_in, lam_q1=lam_q1, lam_k1=lam_k1, lam_q2=lam_q2, lam_k2=lam_k2,
             attn_subln_g=attn_subln_g, lru_conv_w=lru_conv_w, lru_conv_b=lru_conv_b, lru_wa=lru_wa, lru_ba=lru_ba,
             lru_wx=lru_wx, lru_bx=lru_bx, lru_lambda=lru_lambda, ccm_dw_w=ccm_dw_w, ccm_dw_b=ccm_dw_b,
             ccm_ln_g=ccm_ln_g, ccm_ln_b=ccm_ln_b, w_out=w_out, norm_ffn_g=norm_ffn_g,
             router_group_w=router_group_w, router_group_b=router_group_b, router_expert_w=router_expert_w,
             router_expert_b=router_expert_b)
    depth = w_in.shape[0]
    B, S, D = x_prompt.shape
    DB = x_sample.shape[0]
    n_heads = cache_v.shape[3]
    dv = cache_v.shape[4]
    assert dv == HEAD_W and cache_k.shape[4] == HEAD_W
    aw = n_heads * HEAD_W
    cw = lru_conv_w.shape[2]
    n_groups, _, epg = router_expert_w.shape[1:]
    n_experts = n_groups * epg
    scale = (HEAD_W // 2) ** -0.5
    fg = norm_final_g[None]
    params = [_layer_params(l, P) for l in range(depth)]
    lam_inits = [0.8 - 0.6 * math.exp(-0.3 * l) for l in range(depth)]

    T = B * S
    tm = _pick(S, 512)
    tq = _pick(S, 1024)
    tk = _pick(tq, 256)
    tmix = _pick(S, 256)
    tt = _pick(T, 256)
    cap = 2 * T + n_experts * MOE_ROWS
    x = x_prompt.reshape(T, D)
    ks, vs, hs, lcs, ccs = [], [], [], [], []
    for l in range(depth):
        p = params[l]
        qat, qbt, k, v, kb, vt, r4 = _inproj(x, p["nmg"], p["w_in_bf"], aw, scale, tm, B, S)
        att = _attn(p["lamv"], p["subg"].reshape(HEAD_W, 1), qat, qbt, kb, vt, lam_inits[l], tq, tk)
        x1, h2, rec, cnt, hlast, cst = _mix(att, r4, x, p, B, S, tmix, n_groups, epg)
        dest, block_e, n_used = _dispatch_plan(rec, cnt, n_experts, cap)
        xs = _scatter(dest, h2, cap, tt)
        yb = _gmm(block_e, n_used, xs, expert_w_gate, expert_w_up, expert_w_down, l)
        x = _combine(dest, rec, x1, fg, yb, tt, l == depth - 1)
        ks.append(k.reshape(B, S, n_heads, HEAD_W))
        vs.append(v.reshape(B, S, n_heads, HEAD_W))
        hs.append(hlast.reshape(B, cw))
        lt = lru_conv_w.shape[1]
        lcs.append(r4.reshape(B, S, 4 * cw)[:, S - (lt - 1):, 0:cw])
        ct = ccm_dw_w.shape[1]
        ccs.append(cst[:, cst.shape[1] - (ct - 1):, :])
    y_prompt = x.reshape(B, S, D)
    prompt_out = (y_prompt, jnp.stack(ks), jnp.stack(vs), jnp.stack(hs), jnp.stack(lcs), jnp.stack(ccs))

    n_pool, page = cache_k.shape[1], cache_k.shape[2]
    ck = cache_k.reshape(depth, n_pool, page * n_heads, HEAD_W)
    cv = cache_v.reshape(depth, n_pool, page * n_heads, HEAD_W)
    pg = _pick(page_table.shape[1], S_ATTN_PAGES)
    xs_ = x_sample.reshape(DB, D)
    ks, vs, hs, lcs, ccs = [], [], [], [], []
    for l in range(depth):
        p = params[l]
        pj = _s_inproj(xs_, p["nmg"], p["w_in_bf"], _pick(p["w_in_bf"].shape[1], 512))
        q4 = (pj[:, 0:aw] * scale).reshape(DB, n_heads, HEAD_W)
        k4 = pj[:, aw:2 * aw].reshape(DB, n_heads, HEAD_W)
        v4 = pj[:, 2 * aw:3 * aw].reshape(DB, n_heads, HEAD_W)
        dup = lambda a: jnp.concatenate([a, a], axis=1)
        att = _s_attn(page_table, p["lamv"], p["subg"], dup(q4), dup(k4), dup(v4), ck, cv, l, n_heads, lam_inits[l], pg)
        x1, h2, rec, hnew, glu = _s_mix(att.reshape(DB, aw), pj, xs_, state_lru_h[l], state_lru_conv[l],
                                        state_ccm_conv[l], p, aw, n_groups, epg)
        xs_ = _s_moe(h2, rec, x1, fg, expert_w_gate, expert_w_up, expert_w_down, l, l == depth - 1)
        ks.append(k4[:, None])
        vs.append(v4[:, None])
        hs.append(hnew)
        xr = pj[:, 3 * aw:3 * aw + cw]
        lcs.append(jnp.concatenate([state_lru_conv[l][:, 1:], xr[:, None]], axis=1))
        ccs.append(jnp.concatenate([state_ccm_conv[l][:, 1:], glu[:, None]], axis=1))
    y_sample = xs_.reshape(DB, 1, D)
    return (prompt_out[0], y_sample, prompt_out[1], prompt_out[2], prompt_out[3], prompt_out[4], prompt_out[5],
            jnp.stack(ks), jnp.stack(vs), jnp.stack(hs), jnp.stack(lcs), jnp.stack(ccs))
```

```python
import functools
import math

import jax
import jax.numpy as jnp
from jax import lax
from jax.experimental import pallas as pl
from jax.experimental.pallas import tpu as pltpu

F32 = jnp.float32
BF16 = jnp.bfloat16
I32 = jnp.int32
EPS = 1e-6
LRU_C = 8.0
HEAD_W = 128
LANES = 128
SUBLANES = 8
MOE_ROWS = 256
S_ATTN_PAGES = 16


def _cparams(sem, vmem_mb=48):
    return pltpu.CompilerParams(dimension_semantics=sem, vmem_limit_bytes=vmem_mb * 1024 * 1024)


def _full(shape):
    n = len(shape)
    return pl.BlockSpec(shape, lambda *_: (0,) * n)


def _rms(x, g):
    return x * lax.rsqrt(jnp.mean(x * x, axis=-1, keepdims=True) + EPS) * g


def _sigmoid(x):
    return jax.nn.sigmoid(x)


def _diff_lambda(lamv, lam_init):
    a = jnp.sum(lamv[0:1, :] * lamv[1:2, :], axis=-1, keepdims=True)
    b = jnp.sum(lamv[2:3, :] * lamv[3:4, :], axis=-1, keepdims=True)
    return jnp.exp(a) - jnp.exp(b) + lam_init


def _inproj_kernel(x_ref, g_ref, w_ref, qat_ref, qbt_ref, k_ref, v_ref, kb_ref, vt_ref, r_ref, *, scale):
    h = _rms(x_ref[...], g_ref[...]).astype(BF16)
    aw = kb_ref.shape[1]
    tm = kb_ref.shape[0]
    n_heads = aw // HEAD_W

    def mm(lo, hi):
        return jnp.dot(h, w_ref[:, lo:hi], preferred_element_type=F32)

    def rows_by_head(ref, val):
        for hd in range(n_heads):
            ref[pl.ds(hd, tm, stride=n_heads), :] = val[:, hd * HEAD_W:(hd + 1) * HEAD_W]

    q = mm(0, aw) * scale
    lane = lax.broadcasted_iota(I32, q.shape, 1)
    first = (lane & (HEAD_W - 1)) < (HEAD_W // 2)
    qat_ref[...] = jnp.where(first, q, 0.0).T.astype(BF16)
    qbt_ref[...] = jnp.where(first, 0.0, q).T.astype(BF16)
    k = mm(aw, 2 * aw)
    rows_by_head(k_ref, k)
    kb_ref[...] = k.astype(BF16)
    v = mm(2 * aw, 3 * aw)
    rows_by_head(v_ref, v)
    vt_ref[...] = v.T.astype(BF16)
    r_ref[...] = mm(3 * aw, w_ref.shape[1])


def _inproj(x, g, w_bf, aw, scale, tm, B, S):
    T, D = x.shape
    rw = w_bf.shape[1] - 3 * aw
    nt = S // tm
    row = lambda w: pl.BlockSpec((tm, w), lambda i: (i, 0))
    tr = pl.BlockSpec((None, aw, tm), lambda i: (i // nt, 0, i % nt))
    tshape = jax.ShapeDtypeStruct((B, aw, S), BF16)
    n_heads = aw // HEAD_W
    byhead = pl.BlockSpec((tm * n_heads, HEAD_W), lambda i: (i, 0))
    hshape = jax.ShapeDtypeStruct((T * n_heads, HEAD_W), F32)
    return pl.pallas_call(
        functools.partial(_inproj_kernel, scale=scale),
        grid=(T // tm,),
        in_specs=[row(D), _full((1, D)), _full(w_bf.shape)],
        out_specs=[tr, tr, byhead, byhead, row(aw), tr, row(rw)],
        out_shape=[tshape, tshape, hshape, hshape,
                   jax.ShapeDtypeStruct((T, aw), BF16), tshape,
                   jax.ShapeDtypeStruct((T, rw), F32)],
        compiler_params=_cparams(("parallel",)),
        name="inproj",
    )(x, g, w_bf)


def _attn_kernel(lamv_ref, g_ref, qat_ref, qbt_ref, k_ref, vt_ref, o_ref, m_ref, l_ref, acc_ref, *, tq, tk, lam_init):
    i = pl.program_id(2)
    w = jnp.concatenate([qat_ref[...], qbt_ref[...]], axis=1)
    m_ref[...] = jnp.full(m_ref.shape, -jnp.inf, F32)
    l_ref[...] = jnp.zeros(l_ref.shape, F32)
    acc_ref[...] = jnp.zeros(acc_ref.shape, F32)

    nd = tq // tk

    def tiles(j0, masked):
        starts = [pl.multiple_of((j0 + d) * tk, tk) for d in range(nd)]
        ss = [jnp.dot(k_ref[pl.ds(st, tk), :], w, preferred_element_type=F32) for st in starts]
        for st, s in zip(starts, ss):
            if masked:
                kpos = st + lax.broadcasted_iota(I32, s.shape, 0)
                c = lax.broadcasted_iota(I32, s.shape, 1)
                qpos = i * tq + jnp.where(c >= tq, c - tq, c)
                s = jnp.where(kpos <= qpos, s, -jnp.inf)
            m_prev = m_ref[...]
            m_new = jnp.maximum(m_prev, jnp.max(s, axis=0, keepdims=True))
            alpha = jnp.exp(m_prev - m_new)
            p = jnp.exp(s - m_new)
            l_ref[...] = alpha * l_ref[...] + jnp.sum(p, axis=0, keepdims=True)
            vtj = vt_ref[:, pl.ds(st, tk)]
            acc_ref[...] = alpha * acc_ref[...] + jnp.dot(vtj, p.astype(BF16), preferred_element_type=F32)
            m_ref[...] = m_new

    def body(jj, c):
        tiles(jj * nd, False)
        return c

    lax.fori_loop(0, i, body, 0)
    tiles(i * nd, True)

    lam = _diff_lambda(lamv_ref[...], lam_init)
    on = acc_ref[...] / l_ref[...]
    o = on[:, 0:tq] - lam * on[:, tq:2 * tq]
    y = o * lax.rsqrt(jnp.mean(o * o, axis=0, keepdims=True) + EPS) * g_ref[...] * (1.0 - lam_init)
    o_ref[...] = y.T.astype(BF16)


def _attn(lamv, gcol, qat, qbt, kb, vt, lam_init, tq, tk):
    B, aw, S = qat.shape
    H = aw // HEAD_W
    nq = S // tq
    qspec = pl.BlockSpec((None, HEAD_W, tq), lambda b, h, i: (b, h, i))
    return pl.pallas_call(
        functools.partial(_attn_kernel, tq=tq, tk=tk, lam_init=lam_init),
        grid=(B, H, nq),
        in_specs=[_full(lamv.shape), _full((HEAD_W, 1)), qspec, qspec,
                  pl.BlockSpec((S, HEAD_W), lambda b, h, i: (b, h)),
                  pl.BlockSpec((None, HEAD_W, S), lambda b, h, i: (b, h, 0))],
        out_specs=pl.BlockSpec((tq, HEAD_W), lambda b, h, i: (b * nq + i, h)),
        out_shape=jax.ShapeDtypeStruct((B * S, aw), BF16),
        scratch_shapes=[pltpu.VMEM((1, 2 * tq), F32), pltpu.VMEM((1, 2 * tq), F32),
                        pltpu.VMEM((HEAD_W, 2 * tq), F32)],
        compiler_params=_cparams(("parallel", "parallel", "arbitrary")),
        name="attn",
    )(lamv, gcol, qat, qbt, kb, vt)


def _lru_gates(xc, wgate, bgate, llam):
    cw = xc.shape[-1]
    gz = jnp.dot(xc.astype(BF16), wgate, preferred_element_type=F32) + bgate
    r = _sigmoid(gz[:, 0:cw])
    ig = _sigmoid(gz[:, cw:2 * cw])
    log_a = -LRU_C * r * jax.nn.softplus(-llam)
    a = jnp.exp(log_a)
    u = jnp.sqrt(-jnp.tanh(log_a) * (a * a + 1.0)) * ig * xc
    return a, u


def _layer_norm_silu(c, g, b):
    mu = jnp.mean(c, axis=-1, keepdims=True)
    cc = c - mu
    var = jnp.mean(cc * cc, axis=-1, keepdims=True)
    y = cc * lax.rsqrt(var + EPS) * g + b
    return y * _sigmoid(y)


def _route(logits, n_groups, epg):
    lane = lax.broadcasted_iota(I32, logits.shape, 1).astype(F32)
    big = float(LANES)
    ninf = -jnp.inf
    gl = jnp.where(lane < n_groups, logits, ninf)
    gmax = jnp.max(gl, axis=-1, keepdims=True)
    gsel = jnp.min(jnp.where(gl == gmax, lane, big), axis=-1, keepdims=True)
    p_g = 1.0 / jnp.sum(jnp.exp(gl - gmax), axis=-1, keepdims=True)
    lo = n_groups + epg * gsel
    el = jnp.where((lane >= lo) & (lane < lo + epg), logits, ninf)
    v1 = jnp.max(el, axis=-1, keepdims=True)
    i1 = jnp.min(jnp.where(el == v1, lane, big), axis=-1, keepdims=True)
    el2 = jnp.where(lane == i1, ninf, el)
    v2 = jnp.max(el2, axis=-1, keepdims=True)
    i2 = jnp.min(jnp.where(el2 == v2, lane, big), axis=-1, keepdims=True)
    e2 = jnp.exp(v2 - v1)
    den = 1.0 + e2
    g1 = (1.0 / den) * p_g
    g2 = (e2 / den) * p_g
    return i1 - n_groups, i2 - n_groups, g1, g2, lane


def _pack_lanes(lane, vals):
    out = jnp.zeros(lane.shape, F32)
    for idx, v in enumerate(vals):
        out = jnp.where(lane == float(idx), v, out)
    return out


def _mix_kernel(att_ref, r_ref, x_ref, lcw_ref, lcb_ref, wgate_ref, bgate_ref, llam_ref,
                cdw_ref, cdb_ref, clg_ref, clb_ref, wo_ref, nfg_ref, wr_ref, br_ref,
                x1_ref, h2_ref, rec_ref, cnt_ref, hlast_ref, cst_ref,
                xe, ce, cz, a_s, u_s, hs, hc, cnt_s, *, tm, n_groups, epg, lru_taps, ccm_taps):
    b = pl.program_id(0)
    t = pl.program_id(1)
    cw = xe.shape[1]
    lpad = xe.shape[0] - tm
    cpad = ce.shape[0] - tm

    @pl.when(t == 0)
    def _():
        xe[0:lpad, :] = jnp.zeros((lpad, cw), F32)
        ce[0:cpad, :] = jnp.zeros((cpad, cw), F32)
        hc[...] = jnp.zeros(hc.shape, F32)

    @pl.when((t == 0) & (b == 0))
    def _():
        cnt_s[...] = jnp.zeros(cnt_s.shape, F32)

    xg = r_ref[:, cw:2 * cw]
    xe[lpad:lpad + tm, :] = r_ref[:, 0:cw]
    xc = jnp.zeros((tm, cw), F32) + lcb_ref[...]
    for k in range(lru_taps):
        xc = xc + lcw_ref[k:k + 1, :] * xe[pl.ds(lpad - (lru_taps - 1) + k, tm), :]
    xe[0:lpad, :] = xe[tm:tm + lpad, :]
    a, u = _lru_gates(xc, wgate_ref[...], bgate_ref[...], llam_ref[...])
    rowm = lax.broadcasted_iota(I32, (tm, cw), 0) & (SUBLANES - 1)
    for s in (1, 2, 4):
        ok = rowm >= s
        u = jnp.where(ok, a * pltpu.roll(u, s, 0) + u, u)
        a = jnp.where(ok, a * pltpu.roll(a, s, 0), a)
    a_s[...] = a
    u_s[...] = u

    def grp(gi, h):
        o = pl.multiple_of(gi * SUBLANES, SUBLANES)
        h8 = a_s[pl.ds(o, SUBLANES), :] * h + u_s[pl.ds(o, SUBLANES), :]
        hs[pl.ds(o, SUBLANES), :] = h8
        return jnp.broadcast_to(h8[SUBLANES - 1:SUBLANES, :], (SUBLANES, cw))

    hfin = lax.fori_loop(0, tm // SUBLANES, grp, hc[...])
    hc[...] = hfin
    hlast_ref[...] = hfin[0:1, :]
    rec = hs[...] * jax.nn.gelu(xg)

    glu = r_ref[:, 2 * cw:3 * cw] * _sigmoid(r_ref[:, 3 * cw:4 * cw])
    ce[cpad:cpad + tm, :] = glu
    zrows = cz.shape[1]
    for b in range(1, SUBLANES):
        cz[b - 1] = ce[pl.ds(b, zrows), :]
    c = jnp.zeros((tm, cw), F32) + cdb_ref[...]
    for k in range(ccm_taps):
        a8, b = divmod(cpad - (ccm_taps - 1) + k, SUBLANES)
        src = ce if b == 0 else cz.at[b - 1]
        c = c + cdw_ref[k:k + 1, :] * src[pl.ds(a8 * SUBLANES, tm), :]
    ce[0:cpad, :] = ce[tm:tm + cpad, :]
    cst_ref[...] = ce[0:cpad, :]
    cnv = _layer_norm_silu(c, clg_ref[...], clb_ref[...])

    aw = att_ref.shape[1]
    mixed = jnp.dot(att_ref[...], wo_ref[0:aw, :], preferred_element_type=F32)
    mixed += jnp.dot(rec.astype(BF16), wo_ref[aw:aw + cw, :], preferred_element_type=F32)
    mixed += jnp.dot(cnv.astype(BF16), wo_ref[aw + cw:aw + 2 * cw, :], preferred_element_type=F32)
    x1 = x_ref[...] + mixed
    x1_ref[...] = x1

    h2 = _rms(x1, nfg_ref[...])
    h2_ref[...] = h2
    logits = jnp.dot(h2.astype(BF16), wr_ref[...], preferred_element_type=F32) + br_ref[...]
    x1e, x2e, g1, g2, lane = _route(logits, n_groups, epg)
    oh0 = (lane == x1e).astype(F32)
    oh1 = (lane == x2e).astype(F32)
    both = oh0 + oh1
    rr = lax.broadcasted_iota(I32, (tm, tm), 0)
    cc = lax.broadcasted_iota(I32, (tm, tm), 1)
    tril = (cc < rr).astype(BF16)
    before = jnp.dot(tril, both.astype(BF16), preferred_element_type=F32) + cnt_s[0:1, :]
    rank0 = jnp.sum(oh0 * before, axis=-1, keepdims=True)
    rank1 = jnp.sum(oh1 * (before + oh0), axis=-1, keepdims=True)
    cnt_new = cnt_s[0:1, :] + jnp.sum(both, axis=0, keepdims=True)
    cnt_s[...] = jnp.broadcast_to(cnt_new, cnt_s.shape)
    cnt_ref[...] = cnt_s[...]
    rec_ref[...] = _pack_lanes(lane, [x1e, x2e, g1, g2, rank0, rank1])


def _mix(att, r4, x, p, B, S, tm, n_groups, epg):
    T, D = x.shape
    aw = att.shape[1]
    cw = r4.shape[1] // 4
    nt = S // tm
    lru_taps = p["lcw"].shape[0]
    ccm_taps = p["cdw"].shape[0]
    lpad, cpad = SUBLANES, 32
    assert lru_taps - 1 <= lpad and ccm_taps - 1 <= cpad and tm >= cpad
    row = lambda w: pl.BlockSpec((tm, w), lambda b, t: (b * nt + t, 0))
    weights = [p["lcw"], p["lcb"], p["wgate_bf"], p["bgate"], p["llam"], p["cdw"], p["cdb"], p["clg"], p["clb"],
               p["wo_bf"], p["nfg"], p["wr_bf"], p["br"]]
    return pl.pallas_call(
        functools.partial(_mix_kernel, tm=tm, n_groups=n_groups, epg=epg, lru_taps=lru_taps, ccm_taps=ccm_taps),
        grid=(B, nt),
        in_specs=[row(aw), row(4 * cw), row(D)] + [_full(w.shape) for w in weights],
        out_specs=[row(D), row(D), row(LANES), _full((SUBLANES, LANES)),
                   pl.BlockSpec((None, 1, cw), lambda b, t: (b, 0, 0)),
                   pl.BlockSpec((None, cpad, cw), lambda b, t: (b, 0, 0))],
        out_shape=[jax.ShapeDtypeStruct((T, D), F32), jax.ShapeDtypeStruct((T, D), F32),
                   jax.ShapeDtypeStruct((T, LANES), F32), jax.ShapeDtypeStruct((SUBLANES, LANES), F32),
                   jax.ShapeDtypeStruct((B, 1, cw), F32), jax.ShapeDtypeStruct((B, cpad, cw), F32)],
        scratch_shapes=[pltpu.VMEM((lpad + tm, cw), F32), pltpu.VMEM((cpad + tm, cw), F32),
                        pltpu.VMEM((SUBLANES - 1, cpad + tm - SUBLANES, cw), F32),
                        pltpu.VMEM((tm, cw), F32), pltpu.VMEM((tm, cw), F32), pltpu.VMEM((tm, cw), F32),
                        pltpu.VMEM((SUBLANES, cw), F32), pltpu.VMEM((SUBLANES, LANES), F32)],
        compiler_params=_cparams(("arbitrary", "arbitrary")),
        name="mix",
    )(att, r4, x, *weights)


def _scatter_kernel(dest_ref, h2_ref, xs_in_ref, xs_ref, sem, *, tt):
    del xs_in_ref

    def issue(ro, c):
        base = pl.multiple_of(ro * SUBLANES, SUBLANES)
        for ri in range(SUBLANES):
            r = base + ri
            for k in range(2):
                d = dest_ref[2 * r + k]
                pltpu.make_async_copy(h2_ref.at[pl.ds(r, 1), :], xs_ref.at[pl.ds(d, 1), :], sem).start()
        return c

    lax.fori_loop(0, tt // SUBLANES, issue, 0)
    for _ in range(2):
        pltpu.make_async_copy(h2_ref, xs_ref.at[pl.ds(0, tt), :], sem).wait()


def _scatter(dest, h2, cap, tt):
    T, D = h2.shape
    xs0 = jnp.zeros((cap, D), F32)
    return pl.pallas_call(
        functools.partial(_scatter_kernel, tt=tt),
        grid=(T // tt,),
        in_specs=[pl.BlockSpec((2 * tt,), lambda i: (i,), memory_space=pltpu.SMEM),
                  pl.BlockSpec((tt, D), lambda i: (i, 0)),
                  pl.BlockSpec(memory_space=pl.ANY)],
        out_specs=pl.BlockSpec(memory_space=pl.ANY),
        out_shape=jax.ShapeDtypeStruct((cap, D), F32),
        scratch_shapes=[pltpu.SemaphoreType.DMA(())],
        input_output_aliases={2: 0},
        compiler_params=_cparams(("arbitrary",)),
        name="moe_scatter",
    )(dest, h2, xs0)


def _gmm_kernel(be_ref, nu_ref, xs_ref, wg_ref, wu_ref, wd_ref, o_ref, wgb, wub, wdb):
    i = pl.program_id(0)
    e = be_ref[i]
    prev = be_ref[jnp.maximum(i - 1, 0)]

    @pl.when((i == 0) | (e != prev))
    def _():
        wgb[...] = wg_ref[...].astype(BF16)
        wub[...] = wu_ref[...].astype(BF16)
        wdb[...] = wd_ref[...].astype(BF16)

    @pl.when(i < nu_ref[0])
    def _():
        xb = xs_ref[...].astype(BF16)
        g = jnp.dot(xb, wgb[...], preferred_element_type=F32)
        u = jnp.dot(xb, wub[...], preferred_element_type=F32)
        hm = (g * _sigmoid(g) * u).astype(BF16)
        o_ref[...] = jnp.dot(hm, wdb[...], preferred_element_type=F32)

    @pl.when(i >= nu_ref[0])
    def _():
        o_ref[...] = jnp.zeros(o_ref.shape, F32)


def _gmm(block_e, n_used, xs, wg, wu, wd, layer):
    cap, D = xs.shape
    de = wg.shape[3]
    nb = cap // MOE_ROWS
    gs = pltpu.PrefetchScalarGridSpec(
        num_scalar_prefetch=2,
        grid=(nb,),
        in_specs=[pl.BlockSpec((MOE_ROWS, D), lambda i, be, nu: (i, 0)),
                  pl.BlockSpec((None, None, D, de), lambda i, be, nu: (layer, be[i], 0, 0)),
                  pl.BlockSpec((None, None, D, de), lambda i, be, nu: (layer, be[i], 0, 0)),
                  pl.BlockSpec((None, None, de, D), lambda i, be, nu: (layer, be[i], 0, 0))],
        out_specs=pl.BlockSpec((MOE_ROWS, D), lambda i, be, nu: (i, 0)),
        scratch_shapes=[pltpu.VMEM((D, de), BF16), pltpu.VMEM((D, de), BF16), pltpu.VMEM((de, D), BF16)],
    )
    return pl.pallas_call(
        _gmm_kernel,
        grid_spec=gs,
        out_shape=jax.ShapeDtypeStruct((cap, D), F32),
        compiler_params=_cparams(("arbitrary",)),
        name="moe_gmm",
    )(block_e, n_used, xs, wg, wu, wd)


def _combine_kernel(dest_ref, rec_ref, x1_ref, fg_ref, yb_ref, o_ref, buf, sem, *, tt, final_norm):
    def issue(ro, c):
        base = pl.multiple_of(ro * SUBLANES, SUBLANES)
        for ri in range(SUBLANES):
            r = base + ri
            for k in range(2):
                d = dest_ref[2 * r + k]
                pltpu.make_async_copy(yb_ref.at[pl.ds(d, 1), :], buf.at[k, pl.ds(r, 1), :], sem).start()
        return c

    lax.fori_loop(0, tt // SUBLANES, issue, 0)
    for k in range(2):
        pltpu.make_async_copy(yb_ref.at[pl.ds(0, tt), :], buf.at[k], sem).wait()
    g1 = rec_ref[:, 2:3]
    g2 = rec_ref[:, 3:4]
    y = x1_ref[...] + (g1 * buf[0] + g2 * buf[1])
    if final_norm:
        y = _rms(y, fg_ref[...])
    o_ref[...] = y


def _combine(dest, rec, x1, fg, yb, tt, final_norm):
    T, D = x1.shape
    return pl.pallas_call(
        functools.partial(_combine_kernel, tt=tt, final_norm=final_norm),
        grid=(T // tt,),
        in_specs=[pl.BlockSpec((2 * tt,), lambda i: (i,), memory_space=pltpu.SMEM),
                  pl.BlockSpec((tt, LANES), lambda i: (i, 0)),
                  pl.BlockSpec((tt, D), lambda i: (i, 0)),
                  _full((1, D)),
                  pl.BlockSpec(memory_space=pl.ANY)],
        out_specs=pl.BlockSpec((tt, D), lambda i: (i, 0)),
        out_shape=jax.ShapeDtypeStruct((T, D), F32),
        scratch_shapes=[pltpu.VMEM((2, tt, D), F32), pltpu.SemaphoreType.DMA(())],
        compiler_params=_cparams(("arbitrary",)),
        name="moe_combine",
    )(dest, rec, x1, fg, yb)


def _dispatch_plan(rec, cnt, n_experts, cap):
    counts = cnt[0, :n_experts].astype(I32)
    padded = (counts + MOE_ROWS - 1) // MOE_ROWS * MOE_ROWS
    pad_end = jnp.cumsum(padded)
    pad_start = pad_end - padded
    e = rec[:, 0:2].astype(I32)
    rank = rec[:, 4:6].astype(I32)
    dest = (pad_start[e] + rank).reshape(-1)
    nb = cap // MOE_ROWS
    starts = jnp.arange(nb, dtype=I32) * MOE_ROWS
    block_e = jnp.minimum(jnp.sum((pad_end[None, :] <= starts[:, None]).astype(I32), axis=1), n_experts - 1)
    n_used = (pad_end[-1:] // MOE_ROWS).astype(I32)
    return dest, block_e, n_used


def _s_inproj_kernel(x_ref, g_ref, w_ref, o_ref):
    h = _rms(x_ref[...], g_ref[...]).astype(BF16)
    o_ref[...] = jnp.dot(h, w_ref[...], preferred_element_type=F32)


def _s_inproj(x, g, w, tn):
    n, D = x.shape
    W = w.shape[1]
    return pl.pallas_call(
        _s_inproj_kernel,
        grid=(W // tn,),
        in_specs=[_full((n, D)), _full((1, D)), pl.BlockSpec((D, tn), lambda j: (0, j))],
        out_specs=pl.BlockSpec((n, tn), lambda j: (0, j)),
        out_shape=jax.ShapeDtypeStruct((n, W), F32),
        compiler_params=_cparams(("parallel",)),
        name="s_inproj",
    )(x, g, w)


def _s_attn_kernel(pt_ref, lamv_ref, g_ref, q_ref, kn_ref, vn_ref, *rest, n_heads, pg, lam_init):
    del pt_ref
    k_refs = rest[0:pg]
    v_refs = rest[pg:2 * pg]
    o_ref, s_all, m_ref, l_ref, en_ref, acc_ref = rest[2 * pg:]
    j = pl.program_id(1)
    nk = pl.num_programs(1) // 2
    nrow = 2 * n_heads
    rows = lax.broadcasted_iota(I32, (nrow, HEAD_W), 0)
    lanes = lax.broadcasted_iota(I32, (nrow, HEAD_W), 1)
    own = jnp.where(rows >= n_heads, 1, 0) == jnp.where(lanes >= HEAD_W // 2, 1, 0)
    q8 = jnp.where(own, q_ref[...], 0.0)

    qb = q8.astype(BF16)
    prow = k_refs[0].shape[0]
    cols = pg * prow
    lam = _diff_lambda(lamv_ref[...], lam_init)

    @pl.when(j < nk)
    def _():
        @pl.when(j == 0)
        def _():
            m_ref[...] = jnp.full(m_ref.shape, -jnp.inf, F32)

        s_parts = []
        for c in range(pg):
            kp = k_refs[c][...].astype(BF16)
            s_parts.append(lax.dot_general(qb, kp, (((1,), (1,)), ((), ())), preferred_element_type=F32))
        s = jnp.concatenate(s_parts, axis=1)
        r = lax.broadcasted_iota(I32, s.shape, 0)
        col = lax.broadcasted_iota(I32, s.shape, 1)
        s = jnp.where((col & (n_heads - 1)) == (r & (n_heads - 1)), s, -jnp.inf)
        s_all[:, pl.ds(pl.multiple_of(j * cols, cols), cols)] = s
        m_ref[...] = jnp.maximum(m_ref[...], jnp.max(s, axis=-1, keepdims=True))

    @pl.when(j == nk - 1)
    def _():
        sn = jnp.sum(qb.astype(F32) * kn_ref[...].astype(BF16).astype(F32), axis=-1, keepdims=True)
        m = jnp.maximum(m_ref[...], sn)

        def expo(c, l):
            off = pl.multiple_of(c * cols, cols)
            e = jnp.exp(s_all[:, pl.ds(off, cols)] - m)
            s_all[:, pl.ds(off, cols)] = e
            return l + jnp.sum(e, axis=-1, keepdims=True)

        en = jnp.exp(sn - m)
        l_ref[...] = lax.fori_loop(0, nk, expo, jnp.zeros((nrow, 1), F32)) + en
        en_ref[...] = en
        acc_ref[...] = jnp.zeros(acc_ref.shape, F32)

    def weights(e):
        pn = e / l_ref[...]
        return (pn[0:n_heads, :] - lam * pn[n_heads:nrow, :]).astype(BF16)

    @pl.when(j >= nk)
    def _():
        w = weights(s_all[:, pl.ds(pl.multiple_of((j - nk) * cols, cols), cols)])
        pv = jnp.zeros((n_heads, HEAD_W), F32)
        for c in range(pg):
            pv += jnp.dot(w[:, c * prow:(c + 1) * prow], v_refs[c][...].astype(BF16), preferred_element_type=F32)
        acc_ref[...] += pv

    @pl.when(j == 2 * nk - 1)
    def _():
        wn = weights(en_ref[...]).astype(F32)
        o = acc_ref[...] + wn * vn_ref[0:n_heads, :].astype(BF16).astype(F32)
        o_ref[...] = _rms(o, g_ref[...]) * (1.0 - lam_init)


def _s_attn(page_table, lamv, g, q8, kn8, vn8, ck, cv, layer, n_heads, lam_init, pg):
    DB, n_pages = page_table.shape
    nrow = 2 * n_heads
    prow = ck.shape[2]

    nk = n_pages // pg

    def k_spec(c):
        return pl.BlockSpec((None, None, prow, HEAD_W),
                            lambda b, j, pt: (layer, pt[b, jnp.minimum(j, nk - 1) * pg + c], 0, 0))

    def v_spec(c):
        return pl.BlockSpec((None, None, prow, HEAD_W),
                            lambda b, j, pt: (layer, pt[b, jnp.maximum(j - nk, 0) * pg + c], 0, 0))

    tok = pl.BlockSpec((None, nrow, HEAD_W), lambda b, j, pt: (b, 0, 0))
    gs = pltpu.PrefetchScalarGridSpec(
        num_scalar_prefetch=1,
        grid=(DB, 2 * nk),
        in_specs=[pl.BlockSpec(lamv.shape, lambda b, j, pt: (0, 0)),
                  pl.BlockSpec((1, HEAD_W), lambda b, j, pt: (0, 0)),
                  tok, tok, tok] + [k_spec(c) for c in range(pg)] + [v_spec(c) for c in range(pg)],
        out_specs=pl.BlockSpec((None, n_heads, HEAD_W), lambda b, j, pt: (b, 0, 0)),
        scratch_shapes=[pltpu.VMEM((nrow, n_pages * prow), F32), pltpu.VMEM((nrow, 1), F32),
                        pltpu.VMEM((nrow, 1), F32), pltpu.VMEM((nrow, 1), F32), pltpu.VMEM((n_heads, HEAD_W), F32)],
    )
    return pl.pallas_call(
        functools.partial(_s_attn_kernel, n_heads=n_heads, pg=pg, lam_init=lam_init),
        grid_spec=gs,
        out_shape=jax.ShapeDtypeStruct((DB, n_heads, HEAD_W), F32),
        compiler_params=_cparams(("parallel", "arbitrary")),
        name="s_attn",
    )(page_table, lamv, g, q8, kn8, vn8, *([ck] * pg), *([cv] * pg))


def _s_mix_kernel(att_ref, p_ref, x_ref, h0_ref, lst_ref, cst_ref, lcw_ref, lcb_ref, wgate_ref, bgate_ref, llam_ref,
                  cdw_ref, cdb_ref, clg_ref, clb_ref, wo_ref, nfg_ref, wr_ref, br_ref,
                  x1_ref, h2_ref, rec_ref, hnew_ref, glu_ref, *, aw, cw, n_groups, epg):
    base = 3 * aw
    xr = p_ref[:, base:base + cw]
    xg = p_ref[:, base + cw:base + 2 * cw]
    ca = p_ref[:, base + 2 * cw:base + 3 * cw]
    cb = p_ref[:, base + 3 * cw:base + 4 * cw]
    lt = lcw_ref.shape[0]
    xc = jnp.sum(lst_ref[...] * lcw_ref[0:lt - 1, :][None], axis=1) + lcw_ref[lt - 1:lt, :] * xr + lcb_ref[...]
    a, u = _lru_gates(xc, wgate_ref[...], bgate_ref[...], llam_ref[...])
    h = a * h0_ref[...] + u
    hnew_ref[...] = h
    rec = h * jax.nn.gelu(xg)
    glu = ca * _sigmoid(cb)
    glu_ref[...] = glu
    ct = cdw_ref.shape[0]
    c = jnp.sum(cst_ref[...] * cdw_ref[0:ct - 1, :][None], axis=1) + cdw_ref[ct - 1:ct, :] * glu + cdb_ref[...]
    cnv = _layer_norm_silu(c, clg_ref[...], clb_ref[...])
    def dot(a, w):
        return jnp.dot(a.astype(BF16), w, preferred_element_type=F32)

    mixed = dot(att_ref[...], wo_ref[0:aw, :]) + dot(rec, wo_ref[aw:aw + cw, :]) + dot(cnv, wo_ref[aw + cw:aw + 2 * cw, :])
    x1 = x_ref[...] + mixed
    x1_ref[...] = x1
    h2 = _rms(x1, nfg_ref[...])
    h2_ref[...] = h2
    logits = dot(h2, wr_ref[...]) + br_ref[...]
    x1e, x2e, g1, g2, lane = _route(logits, n_groups, epg)
    rec_ref[...] = _pack_lanes(lane, [x1e, x2e, g1, g2])


def _s_mix(att, pj, x, h0, lst, cst, p, aw, n_groups, epg):
    n, D = x.shape
    cw = h0.shape[1]
    ins = [att, pj, x, h0, lst, cst, p["lcw"], p["lcb"], p["wgate_bf"], p["bgate"], p["llam"], p["cdw"], p["cdb"],
           p["clg"], p["clb"], p["wo_bf"], p["nfg"], p["wr_bf"], p["br"]]
    return pl.pallas_call(
        functools.partial(_s_mix_kernel, aw=aw, cw=cw, n_groups=n_groups, epg=epg),
        grid=(1,),
        in_specs=[_full(a.shape) for a in ins],
        out_specs=[_full((n, D)), _full((n, D)), _full((n, LANES)), _full((n, cw)), _full((n, cw))],
        out_shape=[jax.ShapeDtypeStruct((n, D), F32), jax.ShapeDtypeStruct((n, D), F32),
                   jax.ShapeDtypeStruct((n, LANES), F32), jax.ShapeDtypeStruct((n, cw), F32),
                   jax.ShapeDtypeStruct((n, cw), F32)],
        compiler_params=_cparams(("arbitrary",)),
        name="s_mix",
    )(*ins)


def _s_moe_kernel(h2_ref, rec_ref, x1_ref, fg_ref, wg_ref, wu_ref, wd_ref, o_ref, acc, *, final_norm):
    e = pl.program_id(0)

    @pl.when(e == 0)
    def _():
        acc[...] = x1_ref[...]

    ef = e.astype(F32)
    rec = rec_ref[...]
    gate = (jnp.where(rec[:, 0:1] == ef, rec[:, 2:3], 0.0) + jnp.where(rec[:, 1:2] == ef, rec[:, 3:4], 0.0))
    hb = h2_ref[...].astype(BF16)
    g = jnp.dot(hb, wg_ref[...].astype(BF16), preferred_element_type=F32)
    u = jnp.dot(hb, wu_ref[...].astype(BF16), preferred_element_type=F32)
    hm = (g * _sigmoid(g) * u).astype(BF16)
    y = jnp.dot(hm, wd_ref[...].astype(BF16), preferred_element_type=F32)
    acc[...] += jnp.where(gate != 0.0, gate * y, 0.0)

    @pl.when(e == pl.num_programs(0) - 1)
    def _():
        out = acc[...]
        if final_norm:
            out = _rms(out, fg_ref[...])
        o_ref[...] = out


def _s_moe(h2, rec, x1, fg, wg, wu, wd, layer, final_norm):
    n, D = x1.shape
    _, E, _, de = wg.shape
    return pl.pallas_call(
        functools.partial(_s_moe_kernel, final_norm=final_norm),
        grid=(E,),
        in_specs=[_full((n, D)), _full((n, LANES)), _full((n, D)), _full((1, D)),
                  pl.BlockSpec((None, None, D, de), lambda e: (layer, e, 0, 0)),
                  pl.BlockSpec((None, None, D, de), lambda e: (layer, e, 0, 0)),
                  pl.BlockSpec((None, None, de, D), lambda e: (layer, e, 0, 0))],
        out_specs=_full((n, D)),
        out_shape=jax.ShapeDtypeStruct((n, D), F32),
        scratch_shapes=[pltpu.VMEM((n, D), F32)],
        compiler_params=_cparams(("arbitrary",)),
        name="s_moe",
    )(h2, rec, x1, fg, wg, wu, wd)


def _block_diag(w):
    nb, c, d = w.shape
    eye = jnp.eye(nb, dtype=w.dtype)
    return (eye[:, None, :, None] * w[:, :, None, :]).reshape(nb * c, nb * d)


def _layer_params(l, P):
    D = P["w_in"].shape[1]
    n_groups, _, epg = P["router_expert_w"].shape[1:]
    wgate = jnp.concatenate([_block_diag(P["lru_wa"][l]), _block_diag(P["lru_wx"][l])], axis=1)
    wr = jnp.concatenate([P["router_group_w"][l],
                          jnp.moveaxis(P["router_expert_w"][l], 0, 1).reshape(D, n_groups * epg)], axis=1)
    br = jnp.concatenate([P["router_group_b"][l], P["router_expert_b"][l].reshape(-1)])
    padc = LANES - wr.shape[1]
    wr = jnp.pad(wr, ((0, 0), (0, padc)))
    br = jnp.pad(br, (0, padc))[None, :]
    p = dict(
        lcw=P["lru_conv_w"][l], lcb=P["lru_conv_b"][l][None], wgate=wgate,
        bgate=jnp.concatenate([P["lru_ba"][l], P["lru_bx"][l]])[None], llam=P["lru_lambda"][l][None],
        cdw=P["ccm_dw_w"][l], cdb=P["ccm_dw_b"][l][None], clg=P["ccm_ln_g"][l][None], clb=P["ccm_ln_b"][l][None],
        wo=P["w_out"][l], nfg=P["norm_ffn_g"][l][None], wr=wr, br=br,
        lamv=jnp.stack([P["lam_q1"][l], P["lam_k1"][l], P["lam_q2"][l], P["lam_k2"][l]]),
        subg=P["attn_subln_g"][l][None], nmg=P["norm_mix_g"][l][None], w_in=P["w_in"][l],
    )
    p["wgate_bf"] = wgate.astype(BF16)
    p["wo_bf"] = p["wo"].astype(BF16)
    p["wr_bf"] = wr.astype(BF16)
    p["w_in_bf"] = p["w_in"].astype(BF16)
    return p


def _pick(n, pref):
    t = min(n, pref)
    while n % t:
        t //= 2
    return t


def kernel(x_prompt, x_sample, cache_k, cache_v, state_lru_h, state_lru_conv, state_ccm_conv, page_table, norm_mix_g, w_in, lam_q1, lam_k1, lam_q2, lam_k2, attn_subln_g, lru_conv_w, lru_conv_b, lru_wa, lru_ba, lru_wx, lru_bx, lru_lambda, ccm_dw_w, ccm_dw_b, ccm_ln_g, ccm_ln_b, w_out, norm_ffn_g, router_group_w, router_group_b, router_expert_w, router_expert_b, expert_w_gate, expert_w_up, expert_w_down, norm_final_g):
    P = dict(norm_mix_g=norm_mix_g, w_in=w_in, lam_q1=lam_q1, lam_k1=lam_k1, lam_q2=lam_q2, lam_k2=lam_k2,
             attn_subln_g=attn_subln_g, lru_conv_w=lru_conv_w, lru_conv_b=lru_conv_b, lru_wa=lru_wa, lru_ba=lru_ba,
             lru_wx=lru_wx, lru_bx=lru_bx, lru_lambda=lru_lambda, ccm_dw_w=ccm_dw_w, ccm_dw_b=ccm_dw_b,
             ccm_ln_g=ccm_ln_g, ccm_ln_b=ccm_ln_b, w_out=w_out, norm_ffn_g=norm_ffn_g,
             router_group_w=router_group_w, router_group_b=router_group_b, router_expert_w=router_expert_w,
             router_expert_b=router_expert_b)
    depth = w_in.shape[0]
    B, S, D = x_prompt.shape
    DB = x_sample.shape[0]
    n_heads = cache_v.shape[3]
    dv = cache_v.shape[4]
    assert dv == HEAD_W and cache_k.shape[4] == HEAD_W
    aw = n_heads * HEAD_W
    cw = lru_conv_w.shape[2]
    n_groups, _, epg = router_expert_w.shape[1:]
    n_experts = n_groups * epg
    scale = (HEAD_W // 2) ** -0.5
    fg = norm_final_g[None]
    params = [_layer_params(l, P) for l in range(depth)]
    lam_inits = [0.8 - 0.6 * math.exp(-0.3 * l) for l in range(depth)]

    T = B * S
    tm = _pick(S, 512)
    tq = _pick(S, 1024)
    tk = _pick(tq, 256)
    tmix = _pick(S, 256)
    tt = _pick(T, 256)
    cap = 2 * T + n_experts * MOE_ROWS
    x = x_prompt.reshape(T, D)
    ks, vs, hs, lcs, ccs = [], [], [], [], []
    for l in range(depth):
        p = params[l]
        qat, qbt, k, v, kb, vt, r4 = _inproj(x, p["nmg"], p["w_in_bf"], aw, scale, tm, B, S)
        att = _attn(p["lamv"], p["subg"].reshape(HEAD_W, 1), qat, qbt, kb, vt, lam_inits[l], tq, tk)
        x1, h2, rec, cnt, hlast, cst = _mix(att, r4, x, p, B, S, tmix, n_groups, epg)
        dest, block_e, n_used = _dispatch_plan(rec, cnt, n_experts, cap)
        xs = _scatter(dest, h2, cap, tt)
        yb = _gmm(block_e, n_used, xs, expert_w_gate, expert_w_up, expert_w_down, l)
        x = _combine(dest, rec, x1, fg, yb, tt, l == depth - 1)
        ks.append(k.reshape(B, S, n_heads, HEAD_W))
        vs.append(v.reshape(B, S, n_heads, HEAD_W))
        hs.append(hlast.reshape(B, cw))
        lt = lru_conv_w.shape[1]
        lcs.append(r4.reshape(B, S, 4 * cw)[:, S - (lt - 1):, 0:cw])
        ct = ccm_dw_w.shape[1]
        ccs.append(cst[:, cst.shape[1] - (ct - 1):, :])
    y_prompt = x.reshape(B, S, D)
    prompt_out = (y_prompt, jnp.stack(ks), jnp.stack(vs), jnp.stack(hs), jnp.stack(lcs), jnp.stack(ccs))

    n_pool, page = cache_k.shape[1], cache_k.shape[2]
    ck = cache_k.reshape(depth, n_pool, page * n_heads, HEAD_W)
    cv = cache_v.reshape(depth, n_pool, page * n_heads, HEAD_W)
    pg = _pick(page_table.shape[1], S_ATTN_PAGES)
    xs_ = x_sample.reshape(DB, D)
    ks, vs, hs, lcs, ccs = [], [], [], [], []
    for l in range(depth):
        p = params[l]
        pj = _s_inproj(xs_, p["nmg"], p["w_in_bf"], _pick(p["w_in_bf"].shape[1], 512))
        q4 = (pj[:, 0:aw] * scale).reshape(DB, n_heads, HEAD_W)
        k4 = pj[:, aw:2 * aw].reshape(DB, n_heads, HEAD_W)
        v4 = pj[:, 2 * aw:3 * aw].reshape(DB, n_heads, HEAD_W)
        dup = lambda a: jnp.concatenate([a, a], axis=1)
        att = _s_attn(page_table, p["lamv"], p["subg"], dup(q4), dup(k4), dup(v4), ck, cv, l, n_heads, lam_inits[l], pg)
        x1, h2, rec, hnew, glu = _s_mix(att.reshape(DB, aw), pj, xs_, state_lru_h[l], state_lru_conv[l],
                                        state_ccm_conv[l], p, aw, n_groups, epg)
        xs_ = _s_moe(h2, rec, x1, fg, expert_w_gate, expert_w_up, expert_w_down, l, l == depth - 1)
        ks.append(k4[:, None])
        vs.append(v4[:, None])
        hs.append(hnew)
        xr = pj[:, 3 * aw:3 * aw + cw]
        lcs.append(jnp.concatenate([state_lru_conv[l][:, 1:], xr[:, None]], axis=1))
        ccs.append(jnp.concatenate([state_ccm_conv[l][:, 1:], glu[:, None]], axis=1))
    y_sample = xs_.reshape(DB, 1, D)
    return (prompt_out[0], y_sample, prompt_out[1], prompt_out[2], prompt_out[3], prompt_out[4], prompt_out[5],
            jnp.stack(ks), jnp.stack(vs), jnp.stack(hs), jnp.stack(lcs), jnp.stack(ccs))
```
